```python
import jax, jax.numpy as jnp
from jax import lax
import numpy as np

D_MODEL = 2048
BATCH = 1
SEQ = 16384
DEPTH = 1

RET_H = 4
RET_DK = 256
RET_DV = 256
RET_W = RET_H * RET_DK
HG_H = 8
HG_DK = 128
HG_DV = 128
HG_W = HG_H * HG_DK
CHUNK = 128
IN_SIZES = [RET_W] * 4 + [HG_W] * 4 + [D_MODEL] * 2
IN_COLS = sum(IN_SIZES)
D_FF = 5632
CONV_W = 3
ROPE_BASE = 10000.0
NORM_EPS = 1e-6
HEAD_EPS = 1e-5

kernel_name = "hybrid_retention_hgrn2_convffn_adaln"


def rms_norm(x, g, eps=NORM_EPS):
    xf = x.astype(jnp.float32)
    y = xf * lax.rsqrt(jnp.mean(xf * xf, axis=-1, keepdims=True) + eps)
    return (y * g.astype(jnp.float32)).astype(x.dtype)


def modulate(h, shift, scale):
    return h * (1.0 + scale[:, None, :]) + shift[:, None, :]


def rope(t, positions):
    d = t.shape[-1]
    inv_freq = ROPE_BASE ** (-jnp.arange(0, d, 2, dtype=jnp.float32) / d)
    ang = positions.astype(jnp.float32)[..., None] * inv_freq
    cos = jnp.cos(ang)[:, :, None, :]
    sin = jnp.sin(ang)[:, :, None, :]
    t1, t2 = t[..., : d // 2], t[..., d // 2:]
    return jnp.concatenate([t1 * cos - t2 * sin, t2 * cos + t1 * sin], axis=-1)


def to_chunks(t):
    B, S, H, d = t.shape
    return t.reshape(B, S // CHUNK, CHUNK, H, d).transpose(1, 0, 3, 2, 4)


def from_chunks(t):
    n, B, H, C, d = t.shape
    return t.transpose(1, 0, 3, 2, 4).reshape(B, n * C, H, d)


def retention_chunkwise(q, k, v):
    B, S, H, dk = q.shape
    dv = v.shape[-1]
    gamma = 1.0 - jnp.exp2(-5.0 - jnp.arange(H, dtype=jnp.float32))
    log_g = jnp.log(gamma)
    idx = jnp.arange(CHUNK, dtype=jnp.float32)
    diff = idx[:, None] - idx[None, :]
    decay = jnp.where(diff[None] >= 0, jnp.exp(jnp.maximum(diff, 0.0)[None] * log_g[:, None, None]), 0.0)
    xi = jnp.exp((idx[None, :] + 1.0) * log_g[:, None])[:, :, None]
    zeta = jnp.exp((CHUNK - 1.0 - idx[None, :]) * log_g[:, None])[:, :, None]
    g_chunk = jnp.exp(CHUNK * log_g)[:, None, None]
    k = k * (dk ** -0.5)

    def step(R, qkv):
        qc, kc, vc = qkv
        inner = jnp.einsum('bhnd,bhmd->bhnm', qc, kc) * decay
        o = (jnp.einsum('bhnm,bhme->bhne', inner, vc)
             + jnp.einsum('bhnd,bhde->bhne', qc, R) * xi)
        R = g_chunk * R + jnp.einsum('bhmd,bhme->bhde', kc * zeta, vc)
        return R, o

    R0 = jnp.zeros((B, H, dk, dv), jnp.float32)
    _, o = lax.scan(step, R0, (to_chunks(q), to_chunks(k), to_chunks(v)))
    return from_chunks(o)


def hgrn2_chunkwise(q, f, i):
    B, S, H, dk = q.shape
    dv = i.shape[-1]
    k = 1.0 - f
    logf = jnp.log(f)
    t_idx = jnp.arange(CHUNK)
    causal = (t_idx[:, None] >= t_idx[None, :])[:, :, None]

    def step(Sm, inp):
        qc, kc, vc, lf = inp
        b = jnp.cumsum(lf, axis=2)
        dlt = b[:, :, :, None, :] - b[:, :, None, :, :]
        dec = jnp.where(causal, jnp.exp(jnp.minimum(dlt, 0.0)), 0.0)
        A = jnp.einsum('bhtsd,bhsd->bhts', qc[:, :, :, None, :] * dec, kc)
        o = (jnp.einsum('bhts,bhse->bhte', A, vc)
             + jnp.einsum('bhtd,bhde->bhte', qc * jnp.exp(b), Sm))
        b_last = b[:, :, -1:, :]
        Sm = (jnp.exp(b_last[:, :, 0, :])[..., None] * Sm
              + jnp.einsum('bhsd,bhse->bhde', kc * jnp.exp(b_last - b), vc))
        return Sm, o

    S0 = jnp.zeros((B, H, dk, dv), jnp.float32)
    _, o = lax.scan(step, S0, (to_chunks(q), to_chunks(k), to_chunks(i), to_chunks(logf)))
    return from_chunks(o)


def head_layer_norm(o):
    mu = jnp.mean(o, axis=-1, keepdims=True)
    oc = o - mu
    return oc * lax.rsqrt(jnp.mean(oc * oc, axis=-1, keepdims=True) + HEAD_EPS)


def head_rms_norm(o):
    return o * lax.rsqrt(jnp.mean(o * o, axis=-1, keepdims=True) + HEAD_EPS)


def causal_depthwise_conv(u, w, b):
    ch = u.shape[-1]
    y = lax.conv_general_dilated(u, w[:, None, :].astype(u.dtype), window_strides=(1,),
                                 padding=[(CONV_W - 1, 0)],
                                 dimension_numbers=('NWC', 'WIO', 'NWC'),
                                 feature_group_count=ch)
    return y + b.astype(u.dtype)


def setup_inputs(seed: int = 0) -> dict:
    key = jax.random.key(seed)
    ks = jax.random.split(key, 20)
    f32 = jnp.float32
    L = DEPTH
    nrm = lambda k, shape, scale: jax.random.normal(k, shape, f32) * scale
    x = nrm(ks[0], (BATCH, SEQ, D_MODEL), 1.0)
    c = nrm(ks[1], (BATCH, D_MODEL), 1.0)
    offset = jax.random.randint(ks[2], (BATCH, 1), 0, 1024, dtype=jnp.int32)
    positions = offset + jnp.arange(SEQ, dtype=jnp.int32)[None, :]
    return {
        "x": x,
        "c": c,
        "positions": positions,
        "w_ada": nrm(ks[3], (L, D_MODEL, 6 * D_MODEL), D_MODEL ** -0.5),
        "b_ada": nrm(ks[4], (L, 6 * D_MODEL), 0.01),
        "g_norm1": 1.0 + nrm(ks[5], (L, D_MODEL), 0.01),
        "w_in": nrm(ks[6], (L, D_MODEL, IN_COLS), D_MODEL ** -0.5),
        "w_ret_o": nrm(ks[7], (L, RET_H * RET_DV, D_MODEL), (RET_H * RET_DV) ** -0.5),
        "w_hg_o": nrm(ks[8], (L, HG_H * HG_DV, D_MODEL), (HG_H * HG_DV) ** -0.5),
        "w_out": nrm(ks[9], (L, D_MODEL, D_MODEL), D_MODEL ** -0.5),
        "hg_lb": nrm(ks[10], (L + 1, HG_W), 1.0),
        "g_norm2": 1.0 + nrm(ks[11], (L, D_MODEL), 0.01),
        "w_up": nrm(ks[12], (L, D_MODEL, 2 * D_FF), D_MODEL ** -0.5),
        "conv_w": nrm(ks[13], (L, CONV_W, 2 * D_FF), CONV_W ** -0.5),
        "conv_b": nrm(ks[14], (L, 2 * D_FF), 0.01),
        "w_down": nrm(ks[15], (L, D_FF, D_MODEL), D_FF ** -0.5),
        "g_final": 1.0 + nrm(ks[16], (D_MODEL,), 0.01),
    }


def reference(x, c, positions, w_ada, b_ada, g_norm1, w_in, w_ret_o, w_hg_o, w_out,
              hg_lb, g_norm2, w_up, conv_w, conv_b, w_down, g_final):
    B, S, D = x.shape
    f32 = jnp.float32
    split_at = [int(v) for v in np.cumsum(IN_SIZES)[:-1]]
    lb_all = jnp.cumsum(jax.nn.softmax(hg_lb.astype(f32), axis=0), axis=0)

    for l in range(DEPTH):
        mod = jax.nn.silu(c) @ w_ada[l] + b_ada[l]
        shift1, scale1, gate1, shift2, scale2, gate2 = jnp.split(mod, 6, axis=-1)

        h = modulate(rms_norm(x, g_norm1[l]), shift1, scale1)
        p = h @ w_in[l]
        rq, rk, rv, rg, hq, hf, hi, hg, ga, gb = jnp.split(p, split_at, axis=-1)

        q_r = rope(rq.astype(f32).reshape(B, S, RET_H, RET_DK), positions)
        k_r = rope(rk.astype(f32).reshape(B, S, RET_H, RET_DK), positions)
        v_r = rv.astype(f32).reshape(B, S, RET_H, RET_DV)
        o_r = head_layer_norm(retention_chunkwise(q_r, k_r, v_r)).reshape(B, S, RET_H * RET_DV)
        y_a = (o_r.astype(x.dtype) * jax.nn.silu(rg)) @ w_ret_o[l]

        lb = lb_all[l]
        f_g = lb + (1.0 - lb) * jax.nn.sigmoid(hf.astype(f32))
        q_h = jax.nn.silu(hq.astype(f32)).reshape(B, S, HG_H, HG_DK)
        o_h = hgrn2_chunkwise(q_h, f_g.reshape(B, S, HG_H, HG_DK),
                              hi.astype(f32).reshape(B, S, HG_H, HG_DV))
        o_h = head_rms_norm(o_h).reshape(B, S, HG_H * HG_DV)
        y_b = (o_h.astype(x.dtype) * jax.nn.silu(hg)) @ w_hg_o[l]

        merged = jax.nn.sigmoid(ga) * y_a + jax.nn.sigmoid(gb) * y_b
        x = x + gate1[:, None, :] * (merged @ w_out[l])

        h2 = modulate(rms_norm(x, g_norm2[l]), shift2, scale2)
        u = causal_depthwise_conv(h2 @ w_up[l], conv_w[l], conv_b[l])
        a, bgl = jnp.split(u, 2, axis=-1)
        x = x + gate2[:, None, :] * ((jax.nn.silu(a) * bgl) @ w_down[l])

    return rms_norm(x, g_final)
```

```python
import functools

import numpy as np
import jax
import jax.numpy as jnp
from jax import lax
from jax.experimental import pallas as pl
from jax.experimental.pallas import tpu as pltpu

F32 = jnp.float32
BF16 = jnp.bfloat16

RET_H = 4
RET_DK = 256
RET_DV = 256
HG_H = 8
HG_DK = 128
HG_DV = 128
CHUNK = 128
CONV_W = 3
ROPE_BASE = 10000.0
NORM_EPS = 1e-6
HEAD_EPS = 1e-5

V7X_VMEM_BYTES = 64 * 1024 * 1024
VMEM_LIMIT = V7X_VMEM_BYTES - 12 * 1024 * 1024
BF16_SUBLANES = 16

HG_LEVELS = 7


def _silu(t):
    return t * jax.nn.sigmoid(t)


def _dot(a, b):
    return jnp.dot(a, b, preferred_element_type=F32)


def _dot_nt(a, b):
    return lax.dot_general(a, b, (((1,), (1,)), ((), ())), preferred_element_type=F32)


def _dot_tn(a, b):
    return lax.dot_general(a, b, (((0,), (0,)), ((), ())), preferred_element_type=F32)


def _params(n_axes):
    return pltpu.CompilerParams(
        dimension_semantics=("arbitrary",) * n_axes, vmem_limit_bytes=VMEM_LIMIT)


def _mod_kernel(c_ref, w_ref, b_ref, o_ref):
    sc = _silu(c_ref[...])
    o_ref[...] = jnp.sum(sc * w_ref[...], axis=0, keepdims=True) + b_ref[...]


def _mod_call(c_col, w_ada, b_ada, layer):
    d, n = w_ada.shape[1], w_ada.shape[2]
    tn = 1024
    return pl.pallas_call(
        _mod_kernel,
        grid=(n // tn,),
        in_specs=[
            pl.BlockSpec((d, 1), lambda j: (0, 0)),
            pl.BlockSpec((None, d, tn), lambda j: (layer, 0, j)),
            pl.BlockSpec((None, 1, tn), lambda j: (layer, 0, j)),
        ],
        out_specs=pl.BlockSpec((1, tn), lambda j: (0, j)),
        out_shape=jax.ShapeDtypeStruct((1, n), F32),
        compiler_params=_params(1),
        name="mod",
    )(c_col, w_ada, b_ada.reshape(b_ada.shape[0], 1, n))


def _rope_kernel(pos_ref, f_ref, cos_ref, sin_ref):
    ang = pos_ref[...].astype(F32) * f_ref[...]
    cos_ref[...] = jnp.cos(ang)
    sin_ref[...] = jnp.sin(ang)


def _rope_call(pos_col, inv_freq):
    s = pos_col.shape[0]
    hd = inv_freq.shape[1]
    ts = min(s, 2048)
    return pl.pallas_call(
        _rope_kernel,
        grid=(s // ts,),
        in_specs=[pl.BlockSpec((ts, 1), lambda i: (i, 0)),
                  pl.BlockSpec((1, hd), lambda i: (0, 0))],
        out_specs=[pl.BlockSpec((ts, hd), lambda i: (i, 0))] * 2,
        out_shape=[jax.ShapeDtypeStruct((s, hd), F32)] * 2,
        compiler_params=_params(1),
        name="rope_table",
    )(pos_col, inv_freq)


def _norm_mod(x, g, scale, shift):
    y = x * lax.rsqrt(jnp.mean(x * x, axis=-1, keepdims=True) + NORM_EPS)
    return (y * g) * (1.0 + scale) + shift


def _in_kernel(x_ref, g_ref, sc_ref, sh_ref, w_ref, o_ref, h_ref):
    @pl.when(pl.program_id(1) == 0)
    def _():
        h_ref[...] = _norm_mod(x_ref[...], g_ref[...], sc_ref[...], sh_ref[...]).astype(BF16)

    o_ref[...] = _dot(h_ref[...], w_ref[...]).astype(o_ref.dtype)


def _in_call(x, g, scale, shift, w, layer):
    s, d = x.shape
    n = w.shape[2]
    tm = min(s, 1024)
    tn = 1024
    vec = pl.BlockSpec((1, d), lambda i, j: (0, 0))
    return pl.pallas_call(
        _in_kernel,
        grid=(s // tm, n // tn),
        in_specs=[
            pl.BlockSpec((tm, d), lambda i, j: (i, 0)),
            pl.BlockSpec((None, 1, d), lambda i, j: (layer, 0, 0)),
            vec, vec,
            pl.BlockSpec((None, d, tn), lambda i, j: (layer, 0, j)),
        ],
        out_specs=pl.BlockSpec((tm, tn), lambda i, j: (i, j)),
        out_shape=jax.ShapeDtypeStruct((s, n), BF16),
        scratch_shapes=[pltpu.VMEM((tm, d), BF16)],
        compiler_params=_params(2),
        name="in_proj",
    )(x, g.reshape(g.shape[0], 1, d), scale, shift, w)


def _ret_kernel(logg_ref, q_ref, k_ref, v_ref, g_ref, cos_ref, sin_ref, o_ref, r_ref):
    head = pl.program_id(0)

    @pl.when(pl.program_id(1) == 0)
    def _():
        r_ref[...] = jnp.zeros_like(r_ref)

    lg = logg_ref[head]
    cos = cos_ref[...]
    sin = sin_ref[...]
    half = RET_DK // 2

    def rope(t):
        t1 = t[:, :half]
        t2 = t[:, half:]
        return jnp.concatenate([t1 * cos - t2 * sin, t2 * cos + t1 * sin], axis=-1)

    q = rope(q_ref[...].astype(F32))
    k = rope(k_ref[...].astype(F32)) * (RET_DK ** -0.5)
    v = v_ref[...]

    row = lax.broadcasted_iota(jnp.int32, (CHUNK, CHUNK), 0)
    col = lax.broadcasted_iota(jnp.int32, (CHUNK, CHUNK), 1)
    diff = (row - col).astype(F32)
    decay = jnp.where(diff >= 0, jnp.exp(jnp.maximum(diff, 0.0) * lg), 0.0)
    ridx = lax.broadcasted_iota(jnp.int32, (CHUNK, RET_DV), 0).astype(F32)
    xi = jnp.exp((ridx + 1.0) * lg)
    zeta = jnp.exp((CHUNK - 1.0 - ridx) * lg)
    g_chunk = jnp.exp(jnp.full((1, RET_DV), float(CHUNK), F32) * lg)

    qb = q.astype(BF16)
    inner = _dot_nt(qb, k.astype(BF16)) * decay
    r = r_ref[...]
    o = _dot(inner.astype(BF16), v) + _dot(qb, r.astype(BF16)) * xi
    r_ref[...] = r * g_chunk + _dot_tn((k * zeta).astype(BF16), v)

    oc = o - jnp.mean(o, axis=-1, keepdims=True)
    on = oc * lax.rsqrt(jnp.mean(oc * oc, axis=-1, keepdims=True) + HEAD_EPS)
    o_ref[...] = (on * _silu(g_ref[...].astype(F32))).astype(o_ref.dtype)


def _ret_call(p, cos, sin, logg):
    s = p.shape[0]
    nc = s // CHUNK

    def blk(base):
        return pl.BlockSpec((CHUNK, RET_DK), lambda h, c: (c, base + h))

    tab = pl.BlockSpec((CHUNK, RET_DK // 2), lambda h, c: (c, 0))
    return pl.pallas_call(
        _ret_kernel,
        grid=(RET_H, nc),
        in_specs=[pl.BlockSpec(memory_space=pltpu.SMEM),
                  blk(0), blk(RET_H), blk(2 * RET_H), blk(3 * RET_H), tab, tab],
        out_specs=pl.BlockSpec((CHUNK, RET_DV), lambda h, c: (c, h)),
        out_shape=jax.ShapeDtypeStruct((s, RET_H * RET_DV), BF16),
        scratch_shapes=[pltpu.VMEM((RET_DK, RET_DV), F32)],
        compiler_params=_params(2),
        name="retention",
    )(logg, p, p, p, p, cos, sin)


def _hg_tables():
    t = np.arange(CHUNK)
    s = np.arange(CHUNK)
    mats = [(s[None, :] <= t[:, None])]
    upper, pair = [], []
    for l in range(HG_LEVELS):
        w = (CHUNK // 2) >> l
        m = (t // (2 * w)) * (2 * w) + w - 1
        lo = np.minimum(t, m)[:, None]
        hi = np.maximum(t, m)[:, None]
        mats.append((s[None, :] > lo) & (s[None, :] <= hi))
        up = (t // w) % 2 == 1
        upper.append(np.broadcast_to(up[:, None], (CHUNK, HG_DK)))
        pair.append(up[:, None] & ~up[None, :] & ((t[:, None] // (2 * w)) == (s[None, :] // (2 * w))))
    mats.append(s[None, :] > t[:, None])
    pair.append(t[:, None] == s[None, :])
    tcat = np.concatenate(mats, axis=0).astype(np.float32)
    return (jnp.asarray(tcat, BF16), jnp.asarray(np.stack(upper), F32),
            jnp.asarray(np.stack(pair), F32))


def _hg_kernel(layer, lbp_ref, q_ref, f_ref, i_ref, g_ref, tcat_ref, up_ref, pair_ref,
               o_ref, st_ref):
    @pl.when(pl.program_id(1) == 0)
    def _():
        st_ref[...] = jnp.zeros_like(st_ref)

    lbp = lbp_ref[...]
    e = jnp.exp(lbp - jnp.max(lbp, axis=0, keepdims=True))
    lb = jnp.sum(e[:layer + 1], axis=0, keepdims=True) / jnp.sum(e, axis=0, keepdims=True)

    q = _silu(q_ref[...].astype(F32))
    f = lb + (1.0 - lb) * jax.nn.sigmoid(f_ref[...].astype(F32))
    k = 1.0 - f
    logf = jnp.log(f)
    v = i_ref[...]

    l_hi = logf.astype(BF16)
    l_lo = (logf - l_hi.astype(F32)).astype(BF16)
    sums2 = _dot(tcat_ref[...], jnp.concatenate([l_hi, l_lo], axis=1))
    ex = jnp.exp(sums2[:, :HG_DK] + sums2[:, HG_DK:])
    e_b = ex[0:CHUNK]
    e_last = ex[(HG_LEVELS + 1) * CHUNK:]

    ones = jnp.ones((HG_DK, CHUNK), BF16)
    a = pair_ref[HG_LEVELS] * _dot((q * k).astype(BF16), ones)
    qk = q - k
    for l in range(HG_LEVELS):
        x = ((k + up_ref[l] * qk) * ex[(l + 1) * CHUNK:(l + 2) * CHUNK]).astype(BF16)
        a = a + pair_ref[l] * _dot_nt(x, x)

    st = st_ref[...]
    o = _dot(a.astype(BF16), v) + _dot_nt((q * e_b).astype(BF16), st.astype(BF16))
    st_ref[...] = st * e_b[CHUNK - 1:CHUNK, :] + _dot_tn(v, (k * e_last).astype(BF16))

    on = o * lax.rsqrt(jnp.mean(o * o, axis=-1, keepdims=True) + HEAD_EPS)
    o_ref[...] = (on * _silu(g_ref[...].astype(F32))).astype(o_ref.dtype)


def _hg_call(p, hg_lb, layer, col0):
    s = p.shape[0]
    nc = s // CHUNK
    nl = hg_lb.shape[0]
    tcat, upper, pair = _hg_tables()

    def blk(base):
        return pl.BlockSpec((CHUNK, HG_DK), lambda h, c: (c, col0 + base + h))

    def const(arr):
        return pl.BlockSpec(arr.shape, lambda h, c: (0,) * arr.ndim)

    return pl.pallas_call(
        functools.partial(_hg_kernel, layer),
        grid=(HG_H, nc),
        in_specs=[pl.BlockSpec((nl, HG_DK), lambda h, c: (0, h)),
                  blk(0), blk(HG_H), blk(2 * HG_H), blk(3 * HG_H),
                  const(tcat), const(upper), const(pair)],
        out_specs=pl.BlockSpec((CHUNK, HG_DV), lambda h, c: (c, h)),
        out_shape=jax.ShapeDtypeStruct((s, HG_H * HG_DV), BF16),
        scratch_shapes=[pltpu.VMEM((HG_DV, HG_DK), F32)],
        compiler_params=_params(2),
        name="hgrn2",
    )(hg_lb, p, p, p, p, tcat, upper, pair)


def _merge_kernel(a_ref, b_ref, ga_ref, gb_ref, x_ref, gate_ref, wa_ref, wb_ref, wo_ref, o_ref):
    ya = _dot(a_ref[...], wa_ref[...])
    yb = _dot(b_ref[...], wb_ref[...])
    merged = (jax.nn.sigmoid(ga_ref[...].astype(F32)) * ya
              + jax.nn.sigmoid(gb_ref[...].astype(F32)) * yb)
    o_ref[...] = x_ref[...] + gate_ref[...] * _dot(merged.astype(BF16), wo_ref[...])


def _merge_call(oa, ob, p, x, gate, w_ret_o, w_hg_o, w_out, layer, gate_col_block):
    s, d = x.shape
    tm = min(s, 256)

    def full(w):
        return pl.BlockSpec((None,) + w.shape[1:], lambda i: (layer, 0, 0))

    return pl.pallas_call(
        _merge_kernel,
        grid=(s // tm,),
        in_specs=[
            pl.BlockSpec((tm, oa.shape[1]), lambda i: (i, 0)),
            pl.BlockSpec((tm, ob.shape[1]), lambda i: (i, 0)),
            pl.BlockSpec((tm, d), lambda i: (i, gate_col_block)),
            pl.BlockSpec((tm, d), lambda i: (i, gate_col_block + 1)),
            pl.BlockSpec((tm, d), lambda i: (i, 0)),
            pl.BlockSpec((1, d), lambda i: (0, 0)),
            full(w_ret_o), full(w_hg_o), full(w_out),
        ],
        out_specs=pl.BlockSpec((tm, d), lambda i: (i, 0)),
        out_shape=jax.ShapeDtypeStruct((s, d), F32),
        compiler_params=_params(1),
        name="merge",
    )(oa, ob, p, p, x, gate, w_ret_o, w_hg_o, w_out)


def _ffn_kernel(x_ref, g_ref, sc_ref, sh_ref, gate_ref, gf_ref,
                wa_ref, wb_ref, cwa_ref, cwb_ref, cba_ref, cbb_ref, wd_ref,
                o_ref, hx_ref, *, final_norm):
    i = pl.program_id(0)
    j = pl.program_id(1)
    halo = BF16_SUBLANES
    tm = x_ref.shape[0]

    @pl.when(j == 0)
    def _():
        @pl.when(i == 0)
        def _():
            hx_ref[0:halo, :] = jnp.zeros((halo, hx_ref.shape[1]), BF16)

        @pl.when(i > 0)
        def _():
            hx_ref[0:halo, :] = hx_ref[tm:tm + halo, :]

        hx_ref[halo:, :] = _norm_mod(
            x_ref[...], g_ref[...], sc_ref[...], sh_ref[...]).astype(BF16)
        o_ref[...] = jnp.zeros_like(o_ref)

    hx = hx_ref[...]

    def conv(u, cw_ref, cb_ref):
        cw = cw_ref[...]
        y = (cw[0:1] * pltpu.roll(u, 2, 0) + cw[1:2] * pltpu.roll(u, 1, 0)
             + cw[2:3] * u + cb_ref[...])
        return y[halo:, :]

    a = conv(_dot(hx, wa_ref[...]), cwa_ref, cba_ref)
    b = conv(_dot(hx, wb_ref[...]), cwb_ref, cbb_ref)
    o_ref[...] += _dot((_silu(a) * b).astype(BF16), wd_ref[...])

    @pl.when(j == pl.num_programs(1) - 1)
    def _():
        x2 = x_ref[...] + gate_ref[...] * o_ref[...]
        if final_norm:
            x2 = (x2 * lax.rsqrt(jnp.mean(x2 * x2, axis=-1, keepdims=True) + NORM_EPS)
                  * gf_ref[...])
        o_ref[...] = x2


def _ffn_call(x, g, scale, shift, gate, g_final, w_up, conv_w, conv_b, w_down, layer,
              final_norm):
    s, d = x.shape
    dff = w_down.shape[1]
    tm = min(s, 512)
    tf = 512
    nj = dff // tf
    assert CONV_W - 1 <= BF16_SUBLANES
    vec = pl.BlockSpec((1, d), lambda i, j: (0, 0))

    def cols(shape, off):
        return pl.BlockSpec((None,) + shape, lambda i, j: (layer, 0, off + j))

    conv_b3 = conv_b.reshape(conv_b.shape[0], 1, 2 * dff)
    return pl.pallas_call(
        functools.partial(_ffn_kernel, final_norm=final_norm),
        grid=(s // tm, nj),
        in_specs=[
            pl.BlockSpec((tm, d), lambda i, j: (i, 0)),
            pl.BlockSpec((None, 1, d), lambda i, j: (layer, 0, 0)),
            vec, vec, vec, vec,
            cols((d, tf), 0), cols((d, tf), nj),
            cols((CONV_W, tf), 0), cols((CONV_W, tf), nj),
            cols((1, tf), 0), cols((1, tf), nj),
            pl.BlockSpec((None, tf, d), lambda i, j: (layer, j, 0)),
        ],
        out_specs=pl.BlockSpec((tm, d), lambda i, j: (i, 0)),
        out_shape=jax.ShapeDtypeStruct((s, d), F32),
        scratch_shapes=[pltpu.VMEM((tm + BF16_SUBLANES, d), BF16)],
        compiler_params=_params(2),
        name="conv_ffn",
    )(x, g.reshape(g.shape[0], 1, d), scale, shift, gate, g_final.reshape(1, d),
      w_up, w_up, conv_w, conv_w, conv_b3, conv_b3, w_down)


def kernel(x, c, positions, w_ada, b_ada, g_norm1, w_in, w_ret_o, w_hg_o, w_out, hg_lb,
           g_norm2, w_up, conv_w, conv_b, w_down, g_final):
    batch, s, d = x.shape
    depth = w_in.shape[0]
    assert batch == 1 and s % CHUNK == 0
    ret_w = RET_H * RET_DK
    hg_col0 = 4 * ret_w // HG_DK
    gate_col_block = (4 * ret_w + 4 * HG_H * HG_DK) // d

    xs = x.reshape(s, d)
    inv_freq = ROPE_BASE ** (-jnp.arange(0, RET_DK, 2, dtype=F32) / RET_DK)
    cos, sin = _rope_call(positions.reshape(s, 1), inv_freq.reshape(1, RET_DK // 2))
    logg = jnp.log(1.0 - jnp.exp2(-5.0 - jnp.arange(RET_H, dtype=F32)))

    w_in_b = w_in.astype(BF16)
    w_ret_o_b = w_ret_o.astype(BF16)
    w_hg_o_b = w_hg_o.astype(BF16)
    w_out_b = w_out.astype(BF16)
    w_up_b = w_up.astype(BF16)
    w_down_b = w_down.astype(BF16)

    for l in range(depth):
        mod = _mod_call(c.reshape(d, 1), w_ada, b_ada, l)
        shift1, scale1, gate1, shift2, scale2, gate2 = [
            mod[:, n * d:(n + 1) * d] for n in range(6)]
        p = _in_call(xs, g_norm1, scale1, shift1, w_in_b, l)
        o_ret = _ret_call(p, cos, sin, logg)
        o_hg = _hg_call(p, hg_lb, l, hg_col0)
        xs = _merge_call(o_ret, o_hg, p, xs, gate1, w_ret_o_b, w_hg_o_b, w_out_b, l,
                         gate_col_block)
        xs = _ffn_call(xs, g_norm2, scale2, shift2, gate2, g_final, w_up_b, conv_w, conv_b,
                       w_down_b, l, final_norm=(l == depth - 1))
    return xs.reshape(batch, s, d)
```

```python
import functools

import numpy as np
import jax
import jax.numpy as jnp
from jax import lax
from jax.experimental import pallas as pl
from jax.experimental.pallas import tpu as pltpu

F32 = jnp.float32
BF16 = jnp.bfloat16

RET_H = 4
RET_DK = 256
RET_DV = 256
HG_H = 8
HG_DK = 128
HG_DV = 128
CHUNK = 128
CONV_W = 3
ROPE_BASE = 10000.0
NORM_EPS = 1e-6
HEAD_EPS = 1e-5

V7X_VMEM_BYTES = 64 * 1024 * 1024
VMEM_LIMIT = V7X_VMEM_BYTES - 12 * 1024 * 1024
BF16_SUBLANES = 16

HG_LEVELS = 7


def _silu(t):
    return t * jax.nn.sigmoid(t)


def _dot(a, b):
    return jnp.dot(a, b, preferred_element_type=F32)


def _dot_nt(a, b):
    return lax.dot_general(a, b, (((1,), (1,)), ((), ())), preferred_element_type=F32)


def _dot_tn(a, b):
    return lax.dot_general(a, b, (((0,), (0,)), ((), ())), preferred_element_type=F32)


def _params(n_axes):
    return pltpu.CompilerParams(
        dimension_semantics=("arbitrary",) * n_axes, vmem_limit_bytes=VMEM_LIMIT)


def _mod_kernel(c_ref, w_ref, b_ref, o_ref):
    sc = _silu(c_ref[...])
    o_ref[...] = jnp.sum(sc * w_ref[...], axis=0, keepdims=True) + b_ref[...]


def _mod_call(c_col, w_ada, b_ada, layer):
    d, n = w_ada.shape[1], w_ada.shape[2]
    tn = 1024
    return pl.pallas_call(
        _mod_kernel,
        grid=(n // tn,),
        in_specs=[
            pl.BlockSpec((d, 1), lambda j: (0, 0)),
            pl.BlockSpec((None, d, tn), lambda j: (layer, 0, j)),
            pl.BlockSpec((None, 1, tn), lambda j: (layer, 0, j)),
        ],
        out_specs=pl.BlockSpec((1, tn), lambda j: (0, j)),
        out_shape=jax.ShapeDtypeStruct((1, n), F32),
        compiler_params=_params(1),
        name="mod",
    )(c_col, w_ada, b_ada.reshape(b_ada.shape[0], 1, n))


def _rope_kernel(pos_ref, f_ref, cos_ref, sin_ref):
    ang = pos_ref[...].astype(F32) * f_ref[...]
    cos_ref[...] = jnp.cos(ang)
    sin_ref[...] = jnp.sin(ang)


def _rope_call(pos_col, inv_freq):
    s = pos_col.shape[0]
    hd = inv_freq.shape[1]
    ts = min(s, 2048)
    return pl.pallas_call(
        _rope_kernel,
        grid=(s // ts,),
        in_specs=[pl.BlockSpec((ts, 1), lambda i: (i, 0)),
                  pl.BlockSpec((1, hd), lambda i: (0, 0))],
        out_specs=[pl.BlockSpec((ts, hd), lambda i: (i, 0))] * 2,
        out_shape=[jax.ShapeDtypeStruct((s, hd), F32)] * 2,
        compiler_params=_params(1),
        name="rope_table",
    )(pos_col, inv_freq)


def _norm_mod(x, g, scale, shift):
    y = x * lax.rsqrt(jnp.mean(x * x, axis=-1, keepdims=True) + NORM_EPS)
    return (y * g) * (1.0 + scale) + shift


def _in_kernel(x_ref, g_ref, sc_ref, sh_ref, w_ref, o_ref, h_ref):
    @pl.when(pl.program_id(1) == 0)
    def _():
        h_ref[...] = _norm_mod(x_ref[...], g_ref[...], sc_ref[...], sh_ref[...]).astype(BF16)

    o_ref[...] = _dot(h_ref[...], w_ref[...]).astype(o_ref.dtype)


def _in_call(x, g, scale, shift, w, layer):
    s, d = x.shape
    n = w.shape[2]
    tm = min(s, 1024)
    tn = 1024
    vec = pl.BlockSpec((1, d), lambda i, j: (0, 0))
    return pl.pallas_call(
        _in_kernel,
        grid=(s // tm, n // tn),
        in_specs=[
            pl.BlockSpec((tm, d), lambda i, j: (i, 0)),
            pl.BlockSpec((None, 1, d), lambda i, j: (layer, 0, 0)),
            vec, vec,
            pl.BlockSpec((None, d, tn), lambda i, j: (layer, 0, j)),
        ],
        out_specs=pl.BlockSpec((tm, tn), lambda i, j: (i, j)),
        out_shape=jax.ShapeDtypeStruct((s, n), BF16),
        scratch_shapes=[pltpu.VMEM((tm, d), BF16)],
        compiler_params=_params(2),
        name="in_proj",
    )(x, g.reshape(g.shape[0], 1, d), scale, shift, w)


def _ret_tables():
    gamma = 1.0 - jnp.exp2(-5.0 - jnp.arange(RET_H, dtype=F32))
    log_g = jnp.log(gamma)
    idx = jnp.arange(CHUNK, dtype=F32)
    diff = idx[:, None] - idx[None, :]
    decay = jnp.where(diff[None] >= 0,
                      jnp.exp(jnp.maximum(diff, 0.0)[None] * log_g[:, None, None]), 0.0)
    xi = jnp.exp((idx[None, :] + 1.0) * log_g[:, None])[:, :, None]
    zeta = jnp.exp((CHUNK - 1.0 - idx[None, :]) * log_g[:, None])[:, :, None]
    g_chunk = jnp.exp(CHUNK * log_g)[:, None, None]
    return (decay, jnp.broadcast_to(xi, (RET_H, CHUNK, RET_DV)),
            jnp.broadcast_to(zeta, (RET_H, CHUNK, RET_DK)),
            jnp.broadcast_to(g_chunk, (RET_H, 1, RET_DV)))


def _hg_tables():
    t = np.arange(CHUNK)
    s = np.arange(CHUNK)
    mats = [(s[None, :] <= t[:, None])]
    upper, pair = [], []
    for l in range(HG_LEVELS):
        w = (CHUNK // 2) >> l
        m = (t // (2 * w)) * (2 * w) + w - 1
        lo = np.minimum(t, m)[:, None]
        hi = np.maximum(t, m)[:, None]
        mats.append((s[None, :] > lo) & (s[None, :] <= hi))
        up = (t // w) % 2 == 1
        upper.append(np.broadcast_to(up[:, None], (CHUNK, HG_DK)))
        pair.append(up[:, None] & ~up[None, :] & ((t[:, None] // (2 * w)) == (s[None, :] // (2 * w))))
    mats.append(s[None, :] > t[:, None])
    pair.append(t[:, None] == s[None, :])
    tcat = np.concatenate(mats, axis=0).astype(np.float32)
    return (jnp.asarray(tcat, BF16), jnp.asarray(np.stack(upper), F32),
            jnp.asarray(np.stack(pair), F32))


def _ret_head(h, p_ref, cos, sin, dec_ref, xi_ref, zeta_ref, gch_ref, r_ref, o_ref):
    half = RET_DK // 2

    def cols(part):
        c0 = (part * RET_H + h) * RET_DK
        return p_ref[:, c0:c0 + RET_DK]

    def rope(t):
        t1 = t[:, :half]
        t2 = t[:, half:]
        return jnp.concatenate([t1 * cos - t2 * sin, t2 * cos + t1 * sin], axis=-1)

    q = rope(cols(0).astype(F32))
    k = rope(cols(1).astype(F32)) * (RET_DK ** -0.5)
    v = cols(2)

    qb = q.astype(BF16)
    inner = _dot_nt(qb, k.astype(BF16)) * dec_ref[h]
    r = r_ref[h]
    o = _dot(inner.astype(BF16), v) + _dot(qb, r.astype(BF16)) * xi_ref[h]
    r_ref[h] = r * gch_ref[h] + _dot_tn((k * zeta_ref[h]).astype(BF16), v)

    oc = o - jnp.mean(o, axis=-1, keepdims=True)
    on = oc * lax.rsqrt(jnp.mean(oc * oc, axis=-1, keepdims=True) + HEAD_EPS)
    o_ref[:, h * RET_DV:(h + 1) * RET_DV] = (on * _silu(cols(3).astype(F32))).astype(o_ref.dtype)


def _hg_head(h, col0, p_ref, lb, tcat_ref, up_ref, pair_ref, st_ref, o_ref):
    def cols(part):
        c0 = col0 + (part * HG_H + h) * HG_DK
        return p_ref[:, c0:c0 + HG_DK]

    q = _silu(cols(0).astype(F32))
    f = lb + (1.0 - lb) * jax.nn.sigmoid(cols(1).astype(F32))
    k = 1.0 - f
    logf = jnp.log(f)
    v = cols(2)

    l_hi = logf.astype(BF16)
    l_lo = (logf - l_hi.astype(F32)).astype(BF16)
    sums2 = _dot(tcat_ref[...], jnp.concatenate([l_hi, l_lo], axis=1))
    ex = jnp.exp(sums2[:, :HG_DK] + sums2[:, HG_DK:])
    e_b = ex[0:CHUNK]
    e_last = ex[(HG_LEVELS + 1) * CHUNK:]

    ones = jnp.ones((HG_DK, CHUNK), BF16)
    a = pair_ref[HG_LEVELS] * _dot((q * k).astype(BF16), ones)
    qk = q - k
    for l in range(HG_LEVELS):
        x = ((k + up_ref[l] * qk) * ex[(l + 1) * CHUNK:(l + 2) * CHUNK]).astype(BF16)
        a = a + pair_ref[l] * _dot_nt(x, x)

    st = st_ref[h]
    o = _dot(a.astype(BF16), v) + _dot_nt((q * e_b).astype(BF16), st.astype(BF16))
    st_ref[h] = st * e_b[CHUNK - 1:CHUNK, :] + _dot_tn(v, (k * e_last).astype(BF16))

    on = o * lax.rsqrt(jnp.mean(o * o, axis=-1, keepdims=True) + HEAD_EPS)
    o_ref[:, h * HG_DV:(h + 1) * HG_DV] = (on * _silu(cols(3).astype(F32))).astype(o_ref.dtype)


def _mix_kernel(layer, hg_col0, p_ref, cos_ref, sin_ref, dec_ref, xi_ref, zeta_ref, gch_ref,
                lbp_ref, tcat_ref, up_ref, pair_ref, oret_ref, ohg_ref, r_ref, st_ref):
    @pl.when(pl.program_id(0) == 0)
    def _():
        r_ref[...] = jnp.zeros_like(r_ref)
        st_ref[...] = jnp.zeros_like(st_ref)

    lbp = lbp_ref[...]
    e = jnp.exp(lbp - jnp.max(lbp, axis=0, keepdims=True))
    lb_all = jnp.sum(e[:layer + 1], axis=0, keepdims=True) / jnp.sum(e, axis=0, keepdims=True)

    cos = cos_ref[...]
    sin = sin_ref[...]
    for h in range(RET_H):
        _ret_head(h, p_ref, cos, sin, dec_ref, xi_ref, zeta_ref, gch_ref, r_ref, oret_ref)
    for h in range(HG_H):
        _hg_head(h, hg_col0, p_ref, lb_all[:, h * HG_DK:(h + 1) * HG_DK],
                 tcat_ref, up_ref, pair_ref, st_ref, ohg_ref)


def _mix_call(p, cos, sin, hg_lb, layer):
    s = p.shape[0]
    nc = s // CHUNK
    ret_cols = 4 * RET_H * RET_DK
    mix_cols = ret_cols + 4 * HG_H * HG_DK
    tables = _ret_tables() + (hg_lb,) + _hg_tables()

    def const(arr):
        return pl.BlockSpec(arr.shape, lambda c: (0,) * arr.ndim)

    tab = pl.BlockSpec((CHUNK, RET_DK // 2), lambda c: (c, 0))
    return pl.pallas_call(
        functools.partial(_mix_kernel, layer, ret_cols),
        grid=(nc,),
        in_specs=[pl.BlockSpec((CHUNK, mix_cols), lambda c: (c, 0)), tab, tab]
        + [const(t) for t in tables],
        out_specs=[pl.BlockSpec((CHUNK, RET_H * RET_DV), lambda c: (c, 0)),
                   pl.BlockSpec((CHUNK, HG_H * HG_DV), lambda c: (c, 0))],
        out_shape=[jax.ShapeDtypeStruct((s, RET_H * RET_DV), BF16),
                   jax.ShapeDtypeStruct((s, HG_H * HG_DV), BF16)],
        scratch_shapes=[pltpu.VMEM((RET_H, RET_DK, RET_DV), F32),
                        pltpu.VMEM((HG_H, HG_DV, HG_DK), F32)],
        compiler_params=_params(1),
        name="mixers",
    )(p, cos, sin, *tables)


def _merge_kernel(a_ref, b_ref, ga_ref, gb_ref, x_ref, gate_ref, wa_ref, wb_ref, wo_ref, o_ref):
    ya = _dot(a_ref[...], wa_ref[...])
    yb = _dot(b_ref[...], wb_ref[...])
    merged = (jax.nn.sigmoid(ga_ref[...].astype(F32)) * ya
              + jax.nn.sigmoid(gb_ref[...].astype(F32)) * yb)
    o_ref[...] = x_ref[...] + gate_ref[...] * _dot(merged.astype(BF16), wo_ref[...])


def _merge_call(oa, ob, p, x, gate, w_ret_o, w_hg_o, w_out, layer, gate_col_block):
    s, d = x.shape
    tm = min(s, 256)

    def full(w):
        return pl.BlockSpec((None,) + w.shape[1:], lambda i: (layer, 0, 0))

    return pl.pallas_call(
        _merge_kernel,
        grid=(s // tm,),
        in_specs=[
            pl.BlockSpec((tm, oa.shape[1]), lambda i: (i, 0)),
            pl.BlockSpec((tm, ob.shape[1]), lambda i: (i, 0)),
            pl.BlockSpec((tm, d), lambda i: (i, gate_col_block)),
            pl.BlockSpec((tm, d), lambda i: (i, gate_col_block + 1)),
            pl.BlockSpec((tm, d), lambda i: (i, 0)),
            pl.BlockSpec((1, d), lambda i: (0, 0)),
            full(w_ret_o), full(w_hg_o), full(w_out),
        ],
        out_specs=pl.BlockSpec((tm, d), lambda i: (i, 0)),
        out_shape=jax.ShapeDtypeStruct((s, d), F32),
        compiler_params=_params(1),
        name="merge",
    )(oa, ob, p, p, x, gate, w_ret_o, w_hg_o, w_out)


def _ffn_kernel(x_ref, g_ref, sc_ref, sh_ref, gate_ref, gf_ref,
                wa_ref, wb_ref, cwa_ref, cwb_ref, cba_ref, cbb_ref, wd_ref,
                o_ref, hx_ref, *, final_norm):
    i = pl.program_id(0)
    j = pl.program_id(1)
    halo = BF16_SUBLANES
    tm = x_ref.shape[0]

    @pl.when(j == 0)
    def _():
        @pl.when(i == 0)
        def _():
            hx_ref[0:halo, :] = jnp.zeros((halo, hx_ref.shape[1]), BF16)

        @pl.when(i > 0)
        def _():
            hx_ref[0:halo, :] = hx_ref[tm:tm + halo, :]

        hx_ref[halo:, :] = _norm_mod(
            x_ref[...], g_ref[...], sc_ref[...], sh_ref[...]).astype(BF16)
        o_ref[...] = jnp.zeros_like(o_ref)

    hx = hx_ref[...]

    def conv(u, cw_ref, cb_ref):
        cw = cw_ref[...]
        y = (cw[0:1] * pltpu.roll(u, 2, 0) + cw[1:2] * pltpu.roll(u, 1, 0)
             + cw[2:3] * u + cb_ref[...])
        return y[halo:, :]

    a = conv(_dot(hx, wa_ref[...]), cwa_ref, cba_ref)
    b = conv(_dot(hx, wb_ref[...]), cwb_ref, cbb_ref)
    o_ref[...] += _dot((_silu(a) * b).astype(BF16), wd_ref[...])

    @pl.when(j == pl.num_programs(1) - 1)
    def _():
        x2 = x_ref[...] + gate_ref[...] * o_ref[...]
        if final_norm:
            x2 = (x2 * lax.rsqrt(jnp.mean(x2 * x2, axis=-1, keepdims=True) + NORM_EPS)
                  * gf_ref[...])
        o_ref[...] = x2


def _ffn_call(x, g, scale, shift, gate, g_final, w_up, conv_w, conv_b, w_down, layer,
              final_norm):
    s, d = x.shape
    dff = w_down.shape[1]
    tm = min(s, 512)
    tf = 512
    nj = dff // tf
    assert CONV_W - 1 <= BF16_SUBLANES
    vec = pl.BlockSpec((1, d), lambda i, j: (0, 0))

    def cols(shape, off):
        return pl.BlockSpec((None,) + shape, lambda i, j: (layer, 0, off + j))

    conv_b3 = conv_b.reshape(conv_b.shape[0], 1, 2 * dff)
    return pl.pallas_call(
        functools.partial(_ffn_kernel, final_norm=final_norm),
        grid=(s // tm, nj),
        in_specs=[
            pl.BlockSpec((tm, d), lambda i, j: (i, 0)),
            pl.BlockSpec((None, 1, d), lambda i, j: (layer, 0, 0)),
            vec, vec, vec, vec,
            cols((d, tf), 0), cols((d, tf), nj),
            cols((CONV_W, tf), 0), cols((CONV_W, tf), nj),
            cols((1, tf), 0), cols((1, tf), nj),
            pl.BlockSpec((None, tf, d), lambda i, j: (layer, j, 0)),
        ],
        out_specs=pl.BlockSpec((tm, d), lambda i, j: (i, 0)),
        out_shape=jax.ShapeDtypeStruct((s, d), F32),
        scratch_shapes=[pltpu.VMEM((tm + BF16_SUBLANES, d), BF16)],
        compiler_params=_params(2),
        name="conv_ffn",
    )(x, g.reshape(g.shape[0], 1, d), scale, shift, gate, g_final.reshape(1, d),
      w_up, w_up, conv_w, conv_w, conv_b3, conv_b3, w_down)


def kernel(x, c, positions, w_ada, b_ada, g_norm1, w_in, w_ret_o, w_hg_o, w_out, hg_lb,
           g_norm2, w_up, conv_w, conv_b, w_down, g_final):
    batch, s, d = x.shape
    depth = w_in.shape[0]
    assert batch == 1 and s % CHUNK == 0
    ret_w = RET_H * RET_DK
    gate_col_block = (4 * ret_w + 4 * HG_H * HG_DK) // d

    xs = x.reshape(s, d)
    inv_freq = ROPE_BASE ** (-jnp.arange(0, RET_DK, 2, dtype=F32) / RET_DK)
    cos, sin = _rope_call(positions.reshape(s, 1), inv_freq.reshape(1, RET_DK // 2))

    w_in_b = w_in.astype(BF16)
    w_ret_o_b = w_ret_o.astype(BF16)
    w_hg_o_b = w_hg_o.astype(BF16)
    w_out_b = w_out.astype(BF16)
    w_up_b = w_up.astype(BF16)
    w_down_b = w_down.astype(BF16)

    for l in range(depth):
        mod = _mod_call(c.reshape(d, 1), w_ada, b_ada, l)
        shift1, scale1, gate1, shift2, scale2, gate2 = [
            mod[:, n * d:(n + 1) * d] for n in range(6)]
        p = _in_call(xs, g_norm1, scale1, shift1, w_in_b, l)
        o_ret, o_hg = _mix_call(p, cos, sin, hg_lb, l)
        xs = _merge_call(o_ret, o_hg, p, xs, gate1, w_ret_o_b, w_hg_o_b, w_out_b, l,
                         gate_col_block)
        xs = _ffn_call(xs, g_norm2, scale2, shift2, gate2, g_final, w_up_b, conv_w, conv_b,
                       w_down_b, l, final_norm=(l == depth - 1))
    return xs.reshape(batch, s, d)
```

```python
import functools

import numpy as np
import jax
import jax.numpy as jnp
from jax import lax
from jax.experimental import pallas as pl
from jax.experimental.pallas import tpu as pltpu

F32 = jnp.float32
BF16 = jnp.bfloat16

RET_H = 4
RET_DK = 256
RET_DV = 256
HG_H = 8
HG_DK = 128
HG_DV = 128
CHUNK = 128
CONV_W = 3
ROPE_BASE = 10000.0
NORM_EPS = 1e-6
HEAD_EPS = 1e-5

V7X_VMEM_BYTES = 64 * 1024 * 1024
VMEM_LIMIT = V7X_VMEM_BYTES - 8 * 1024 * 1024
BF16_SUBLANES = 16

F32_SUBLANES = 8
HG_LEVELS = 7
HG_MXU_LEVELS = (5,)


def _silu(t):
    return t * jax.nn.sigmoid(t)


def _dot(a, b):
    return jnp.dot(a, b, preferred_element_type=F32)


def _dot_nt(a, b):
    return lax.dot_general(a, b, (((1,), (1,)), ((), ())), preferred_element_type=F32)


def _dot_tn(a, b):
    return lax.dot_general(a, b, (((0,), (0,)), ((), ())), preferred_element_type=F32)


def _params(n_axes):
    return pltpu.CompilerParams(
        dimension_semantics=("arbitrary",) * n_axes, vmem_limit_bytes=VMEM_LIMIT)


def _mod_kernel(c_ref, w_ref, b_ref, o_ref):
    sc = _silu(c_ref[...])
    o_ref[...] = jnp.sum(sc * w_ref[...], axis=0, keepdims=True) + b_ref[...]


def _mod_call(c_col, w_ada, b_ada, layer):
    d, n = w_ada.shape[1], w_ada.shape[2]
    tn = 1024
    return pl.pallas_call(
        _mod_kernel,
        grid=(n // tn,),
        in_specs=[
            pl.BlockSpec((d, 1), lambda j: (0, 0)),
            pl.BlockSpec((None, d, tn), lambda j: (layer, 0, j)),
            pl.BlockSpec((None, 1, tn), lambda j: (layer, 0, j)),
        ],
        out_specs=pl.BlockSpec((1, tn), lambda j: (0, j)),
        out_shape=jax.ShapeDtypeStruct((1, n), F32),
        compiler_params=_params(1),
        name="mod",
    )(c_col, w_ada, b_ada.reshape(b_ada.shape[0], 1, n))


def _rope_kernel(pos_ref, f_ref, cos_ref, sin_ref):
    ang = pos_ref[...].astype(F32) * f_ref[...]
    cos_ref[...] = jnp.cos(ang)
    sin_ref[...] = jnp.sin(ang)


def _rope_call(pos_col, inv_freq):
    s = pos_col.shape[0]
    hd = inv_freq.shape[1]
    ts = min(s, 2048)
    return pl.pallas_call(
        _rope_kernel,
        grid=(s // ts,),
        in_specs=[pl.BlockSpec((ts, 1), lambda i: (i, 0)),
                  pl.BlockSpec((1, hd), lambda i: (0, 0))],
        out_specs=[pl.BlockSpec((ts, hd), lambda i: (i, 0))] * 2,
        out_shape=[jax.ShapeDtypeStruct((s, hd), F32)] * 2,
        compiler_params=_params(1),
        name="rope_table",
    )(pos_col, inv_freq)


def _norm_mod(x, g, scale, shift):
    y = x * lax.rsqrt(jnp.mean(x * x, axis=-1, keepdims=True) + NORM_EPS)
    return (y * g) * (1.0 + scale) + shift


def _in_kernel(x_ref, g_ref, sc_ref, sh_ref, w_ref, o_ref, h_ref):
    @pl.when(pl.program_id(1) == 0)
    def _():
        h_ref[...] = _norm_mod(x_ref[...], g_ref[...], sc_ref[...], sh_ref[...]).astype(BF16)

    o_ref[...] = _dot(h_ref[...], w_ref[...]).astype(o_ref.dtype)


def _in_call(x, g, scale, shift, w, layer):
    s, d = x.shape
    n = w.shape[2]
    tm = min(s, 1024)
    tn = 1024
    vec = pl.BlockSpec((1, d), lambda i, j: (0, 0))
    return pl.pallas_call(
        _in_kernel,
        grid=(s // tm, n // tn),
        in_specs=[
            pl.BlockSpec((tm, d), lambda i, j: (i, 0)),
            pl.BlockSpec((None, 1, d), lambda i, j: (layer, 0, 0)),
            vec, vec,
            pl.BlockSpec((None, d, tn), lambda i, j: (layer, 0, j)),
        ],
        out_specs=pl.BlockSpec((tm, tn), lambda i, j: (i, j)),
        out_shape=jax.ShapeDtypeStruct((s, n), BF16),
        scratch_shapes=[pltpu.VMEM((tm, d), BF16)],
        compiler_params=_params(2),
        name="in_proj",
    )(x, g.reshape(g.shape[0], 1, d), scale, shift, w)


def _ret_tables():
    gamma = 1.0 - jnp.exp2(-5.0 - jnp.arange(RET_H, dtype=F32))
    log_g = jnp.log(gamma)
    idx = jnp.arange(CHUNK, dtype=F32)
    diff = idx[:, None] - idx[None, :]
    decay = jnp.where(diff[None] >= 0,
                      jnp.exp(jnp.maximum(diff, 0.0)[None] * log_g[:, None, None]), 0.0)
    xi = jnp.exp((idx[None, :] + 1.0) * log_g[:, None])[:, :, None]
    zeta = jnp.exp((CHUNK - 1.0 - idx[None, :]) * log_g[:, None])[:, :, None]
    g_chunk = jnp.exp(CHUNK * log_g)[:, None, None]
    return (decay, jnp.broadcast_to(xi, (RET_H, CHUNK, RET_DV)),
            jnp.broadcast_to(zeta, (RET_H, CHUNK, RET_DK)),
            jnp.broadcast_to(g_chunk, (RET_H, 1, RET_DV)))


def _hg_tables():
    t = np.arange(CHUNK)
    sums = [t[None, :] <= t[:, None]]
    upper, pair = [], []
    for l in range(HG_LEVELS):
        w = (CHUNK // 2) >> l
        up = (t // w) % 2 == 1
        upper.append(np.broadcast_to(up[:, None], (CHUNK, HG_DK)))
        pair.append(up[:, None] & ~up[None, :] & ((t[:, None] // (2 * w)) == (t[None, :] // (2 * w))))
        if l in HG_MXU_LEVELS:
            m = (t // (2 * w)) * (2 * w) + w - 1
            lo = np.minimum(t, m)[:, None]
            hi = np.maximum(t, m)[:, None]
            sums.append((t[None, :] > lo) & (t[None, :] <= hi))
    pair.append(t[:, None] == t[None, :])
    tsum = np.concatenate(sums, axis=0).astype(np.float32)
    return (jnp.asarray(tsum, BF16), jnp.asarray(np.stack(upper), F32),
            jnp.asarray(np.stack(pair), F32))


def _ret_head(h, p_ref, cos, sin, dec_ref, xi_ref, zeta_ref, gch_ref, r_ref, o_ref):
    half = RET_DK // 2

    def cols(part):
        c0 = (part * RET_H + h) * RET_DK
        return p_ref[:, c0:c0 + RET_DK]

    def rope(t):
        t1 = t[:, :half]
        t2 = t[:, half:]
        return jnp.concatenate([t1 * cos - t2 * sin, t2 * cos + t1 * sin], axis=-1)

    q = rope(cols(0).astype(F32))
    k = rope(cols(1).astype(F32)) * (RET_DK ** -0.5)
    v = cols(2)

    qb = q.astype(BF16)
    inner = _dot_nt(qb, k.astype(BF16)) * dec_ref[h]
    r = r_ref[h]
    o = _dot(inner.astype(BF16), v) + _dot(qb, r.astype(BF16)) * xi_ref[h]
    r_ref[h] = r * gch_ref[h] + _dot_tn((k * zeta_ref[h]).astype(BF16), v)

    oc = o - jnp.mean(o, axis=-1, keepdims=True)
    on = oc * lax.rsqrt(jnp.mean(oc * oc, axis=-1, keepdims=True) + HEAD_EPS)
    o_ref[:, h * RET_DV:(h + 1) * RET_DV] = (on * _silu(cols(3).astype(F32))).astype(o_ref.dtype)


def _hg_head(h, col0, p_ref, lb, tsum_ref, up_ref, pair_ref, st_ref, o_ref):
    def cols(part):
        c0 = col0 + (part * HG_H + h) * HG_DK
        return p_ref[:, c0:c0 + HG_DK]

    q = _silu(cols(0).astype(F32))
    f = lb + (1.0 - lb) * jax.nn.sigmoid(cols(1).astype(F32))
    k = 1.0 - f
    v = cols(2)

    l2 = jnp.log2(f)
    l_hi = l2.astype(BF16)
    l_lo = (l2 - l_hi.astype(F32)).astype(BF16)
    sums2 = _dot(tsum_ref[...], jnp.concatenate([l_hi, l_lo], axis=1))
    sums = sums2[:, :HG_DK] + sums2[:, HG_DK:]
    b = sums[0:CHUNK]

    def rows(t, r0, n):
        return t[r0:r0 + n]

    g = F32_SUBLANES
    diag = pair_ref[HG_LEVELS] * jnp.sum(q * k, axis=-1, keepdims=True)
    a_rows = [rows(diag, r, g) for r in range(0, CHUNK, g)]
    qk = q - k
    mxu_level = 0
    for l in range(HG_LEVELS):
        w = (CHUNK // 2) >> l
        if w >= g:
            xs, ups = [], []
            for r0 in range(0, CHUNK, 2 * w):
                b_mid = rows(b, r0 + w - 1, 1)
                xs.append(rows(k, r0, w) * jnp.exp2(b_mid - rows(b, r0, w)))
                xs.append(rows(q, r0 + w, w) * jnp.exp2(rows(b, r0 + w, w) - b_mid))
                ups.append(xs[-1])
            x = jnp.concatenate(xs, axis=0).astype(BF16)
            if w >= BF16_SUBLANES:
                z = _dot_nt(jnp.concatenate(ups, axis=0).astype(BF16), x)
                up_starts = [r0 + w for r0 in range(0, CHUNK, 2 * w)]
                for n, r0 in enumerate(up_starts):
                    for r in range(0, w, g):
                        a_rows[(r0 + r) // g] = (a_rows[(r0 + r) // g]
                                                 + pair_ref[l, r0 + r:r0 + r + g, :]
                                                 * rows(z, n * w + r, g))
                continue
        elif l in HG_MXU_LEVELS:
            mxu_level += 1
            x = ((k + up_ref[l] * qk) * jnp.exp2(rows(sums, mxu_level * CHUNK, CHUNK))).astype(BF16)
        elif 2 * w == g:
            b_mid = jnp.concatenate(
                [jnp.broadcast_to(rows(b, r0 + w - 1, 1), (2 * w, HG_DK))
                 for r0 in range(0, CHUNK, 2 * w)], axis=0)
            sign = 2.0 * up_ref[l] - 1.0
            x = ((k + up_ref[l] * qk) * jnp.exp2((b - b_mid) * sign)).astype(BF16)
        else:
            assert w == 1
            x = (k + up_ref[l] * (q * f - k)).astype(BF16)
        z = _dot_nt(x, x)
        for r in range(0, CHUNK, g):
            a_rows[r // g] = a_rows[r // g] + pair_ref[l, r:r + g, :] * rows(z, r, g)
    a = jnp.concatenate(a_rows, axis=0)

    e_b = jnp.exp2(b)
    e_last = jnp.exp2(rows(b, CHUNK - 1, 1) - b)
    st = st_ref[h]
    o = _dot(a.astype(BF16), v) + _dot_nt((q * e_b).astype(BF16), st.astype(BF16))
    st_ref[h] = st * rows(e_b, CHUNK - 1, 1) + _dot_tn(v, (k * e_last).astype(BF16))

    on = o * lax.rsqrt(jnp.mean(o * o, axis=-1, keepdims=True) + HEAD_EPS)
    o_ref[:, h * HG_DV:(h + 1) * HG_DV] = (on * _silu(cols(3).astype(F32))).astype(o_ref.dtype)


def _mix_kernel(layer, hg_col0, p_ref, cos_ref, sin_ref, dec_ref, xi_ref, zeta_ref, gch_ref,
                lbp_ref, tsum_ref, up_ref, pair_ref, oret_ref, ohg_ref, r_ref, st_ref):
    @pl.when(pl.program_id(0) == 0)
    def _():
        r_ref[...] = jnp.zeros_like(r_ref)
        st_ref[...] = jnp.zeros_like(st_ref)

    lbp = lbp_ref[...]
    e = jnp.exp(lbp - jnp.max(lbp, axis=0, keepdims=True))
    lb_all = jnp.sum(e[:layer + 1], axis=0, keepdims=True) / jnp.sum(e, axis=0, keepdims=True)

    cos = cos_ref[...]
    sin = sin_ref[...]
    for h in range(RET_H):
        _ret_head(h, p_ref, cos, sin, dec_ref, xi_ref, zeta_ref, gch_ref, r_ref, oret_ref)
    for h in range(HG_H):
        _hg_head(h, hg_col0, p_ref, lb_all[:, h * HG_DK:(h + 1) * HG_DK],
                 tsum_ref, up_ref, pair_ref, st_ref, ohg_ref)


def _mix_call(p, cos, sin, hg_lb, layer):
    s = p.shape[0]
    nc = s // CHUNK
    ret_cols = 4 * RET_H * RET_DK
    mix_cols = ret_cols + 4 * HG_H * HG_DK
    tables = _ret_tables() + (hg_lb,) + _hg_tables()

    def const(arr):
        return pl.BlockSpec(arr.shape, lambda c: (0,) * arr.ndim)

    tab = pl.BlockSpec((CHUNK, RET_DK // 2), lambda c: (c, 0))
    return pl.pallas_call(
        functools.partial(_mix_kernel, layer, ret_cols),
        grid=(nc,),
        in_specs=[pl.BlockSpec((CHUNK, mix_cols), lambda c: (c, 0)), tab, tab]
        + [const(t) for t in tables],
        out_specs=[pl.BlockSpec((CHUNK, RET_H * RET_DV), lambda c: (c, 0)),
                   pl.BlockSpec((CHUNK, HG_H * HG_DV), lambda c: (c, 0))],
        out_shape=[jax.ShapeDtypeStruct((s, RET_H * RET_DV), BF16),
                   jax.ShapeDtypeStruct((s, HG_H * HG_DV), BF16)],
        scratch_shapes=[pltpu.VMEM((RET_H, RET_DK, RET_DV), F32),
                        pltpu.VMEM((HG_H, HG_DV, HG_DK), F32)],
        compiler_params=_params(1),
        name="mixers",
    )(p, cos, sin, *tables)


def _merge_kernel(a_ref, b_ref, ga_ref, gb_ref, x_ref, gate_ref, wa_ref, wb_ref, wo_ref, o_ref):
    ya = _dot(a_ref[...], wa_ref[...])
    yb = _dot(b_ref[...], wb_ref[...])
    merged = (jax.nn.sigmoid(ga_ref[...].astype(F32)) * ya
              + jax.nn.sigmoid(gb_ref[...].astype(F32)) * yb)
    o_ref[...] = x_ref[...] + gate_ref[...] * _dot(merged.astype(BF16), wo_ref[...])


def _merge_call(oa, ob, p, x, gate, w_ret_o, w_hg_o, w_out, layer, gate_col_block):
    s, d = x.shape
    tm = min(s, 256)

    def full(w):
        return pl.BlockSpec((None,) + w.shape[1:], lambda i: (layer, 0, 0))

    return pl.pallas_call(
        _merge_kernel,
        grid=(s // tm,),
        in_specs=[
            pl.BlockSpec((tm, oa.shape[1]), lambda i: (i, 0)),
            pl.BlockSpec((tm, ob.shape[1]), lambda i: (i, 0)),
            pl.BlockSpec((tm, d), lambda i: (i, gate_col_block)),
            pl.BlockSpec((tm, d), lambda i: (i, gate_col_block + 1)),
            pl.BlockSpec((tm, d), lambda i: (i, 0)),
            pl.BlockSpec((1, d), lambda i: (0, 0)),
            full(w_ret_o), full(w_hg_o), full(w_out),
        ],
        out_specs=pl.BlockSpec((tm, d), lambda i: (i, 0)),
        out_shape=jax.ShapeDtypeStruct((s, d), F32),
        compiler_params=_params(1),
        name="merge",
    )(oa, ob, p, p, x, gate, w_ret_o, w_hg_o, w_out)


def _ffn_kernel(x_ref, g_ref, sc_ref, sh_ref, gate_ref, gf_ref,
                wa_ref, wb_ref, cwa_ref, cwb_ref, cba_ref, cbb_ref, wd_ref,
                o_ref, hx_ref, *, final_norm):
    i = pl.program_id(0)
    j = pl.program_id(1)
    halo = BF16_SUBLANES
    tm = x_ref.shape[0]

    @pl.when(j == 0)
    def _():
        @pl.when(i == 0)
        def _():
            hx_ref[0:halo, :] = jnp.zeros((halo, hx_ref.shape[1]), BF16)

        @pl.when(i > 0)
        def _():
            hx_ref[0:halo, :] = hx_ref[tm:tm + halo, :]

        hx_ref[halo:, :] = _norm_mod(
            x_ref[...], g_ref[...], sc_ref[...], sh_ref[...]).astype(BF16)
        o_ref[...] = jnp.zeros_like(o_ref)

    hx = hx_ref[...]

    def conv(u, cw_ref, cb_ref):
        cw = cw_ref[...]
        y = (cw[0:1] * pltpu.roll(u, 2, 0) + cw[1:2] * pltpu.roll(u, 1, 0)
             + cw[2:3] * u + cb_ref[...])
        return y[halo:, :]

    a = conv(_dot(hx, wa_ref[...]), cwa_ref, cba_ref)
    b = conv(_dot(hx, wb_ref[...]), cwb_ref, cbb_ref)
    o_ref[...] += _dot((_silu(a) * b).astype(BF16), wd_ref[...])

    @pl.when(j == pl.num_programs(1) - 1)
    def _():
        x2 = x_ref[...] + gate_ref[...] * o_ref[...]
        if final_norm:
            x2 = (x2 * lax.rsqrt(jnp.mean(x2 * x2, axis=-1, keepdims=True) + NORM_EPS)
                  * gf_ref[...])
        o_ref[...] = x2


def _ffn_call(x, g, scale, shift, gate, g_final, w_up, conv_w, conv_b, w_down, layer,
              final_norm):
    s, d = x.shape
    dff = w_down.shape[1]
    tm = min(s, 1024)
    tf = 512
    nj = dff // tf
    assert CONV_W - 1 <= BF16_SUBLANES
    vec = pl.BlockSpec((1, d), lambda i, j: (0, 0))
    x_spec = pl.BlockSpec((tm, d), lambda i, j: (i, 0), pipeline_mode=pl.Buffered(1))

    def cols(shape, off):
        return pl.BlockSpec((None,) + shape, lambda i, j: (layer, 0, off + j))

    conv_b3 = conv_b.reshape(conv_b.shape[0], 1, 2 * dff)
    return pl.pallas_call(
        functools.partial(_ffn_kernel, final_norm=final_norm),
        grid=(s // tm, nj),
        in_specs=[
            x_spec,
            pl.BlockSpec((None, 1, d), lambda i, j: (layer, 0, 0)),
            vec, vec, vec, vec,
            cols((d, tf), 0), cols((d, tf), nj),
            cols((CONV_W, tf), 0), cols((CONV_W, tf), nj),
            cols((1, tf), 0), cols((1, tf), nj),
            pl.BlockSpec((None, tf, d), lambda i, j: (layer, j, 0)),
        ],
        out_specs=pl.BlockSpec((tm, d), lambda i, j: (i, 0)),
        out_shape=jax.ShapeDtypeStruct((s, d), F32),
        scratch_shapes=[pltpu.VMEM((tm + BF16_SUBLANES, d), BF16)],
        compiler_params=_params(2),
        name="conv_ffn",
    )(x, g.reshape(g.shape[0], 1, d), scale, shift, gate, g_final.reshape(1, d),
      w_up, w_up, conv_w, conv_w, conv_b3, conv_b3, w_down)


def kernel(x, c, positions, w_ada, b_ada, g_norm1, w_in, w_ret_o, w_hg_o, w_out, hg_lb,
           g_norm2, w_up, conv_w, conv_b, w_down, g_final):
    batch, s, d = x.shape
    depth = w_in.shape[0]
    assert batch == 1 and s % CHUNK == 0
    ret_w = RET_H * RET_DK
    gate_col_block = (4 * ret_w + 4 * HG_H * HG_DK) // d

    xs = x.reshape(s, d)
    inv_freq = ROPE_BASE ** (-jnp.arange(0, RET_DK, 2, dtype=F32) / RET_DK)
    cos, sin = _rope_call(positions.reshape(s, 1), inv_freq.reshape(1, RET_DK // 2))

    w_in_b = w_in.astype(BF16)
    w_ret_o_b = w_ret_o.astype(BF16)
    w_hg_o_b = w_hg_o.astype(BF16)
    w_out_b = w_out.astype(BF16)
    w_up_b = w_up.astype(BF16)
    w_down_b = w_down.astype(BF16)

    for l in range(depth):
        mod = _mod_call(c.reshape(d, 1), w_ada, b_ada, l)
        shift1, scale1, gate1, shift2, scale2, gate2 = [
            mod[:, n * d:(n + 1) * d] for n in range(6)]
        p = _in_call(xs, g_norm1, scale1, shift1, w_in_b, l)
        o_ret, o_hg = _mix_call(p, cos, sin, hg_lb, l)
        xs = _merge_call(o_ret, o_hg, p, xs, gate1, w_ret_o_b, w_hg_o_b, w_out_b, l,
                         gate_col_block)
        xs = _ffn_call(xs, g_norm2, scale2, shift2, gate2, g_final, w_up_b, conv_w, conv_b,
                       w_down_b, l, final_norm=(l == depth - 1))
    return xs.reshape(batch, s, d)
```

```python
import functools

import numpy as np
import jax
import jax.numpy as jnp
from jax import lax
from jax.experimental import pallas as pl
from jax.experimental.pallas import tpu as pltpu

F32 = jnp.float32
BF16 = jnp.bfloat16

RET_H = 4
RET_DK = 256
RET_DV = 256
HG_H = 8
HG_DK = 128
HG_DV = 128
CHUNK = 128
CONV_W = 3
ROPE_BASE = 10000.0
NORM_EPS = 1e-6
HEAD_EPS = 1e-5

V7X_VMEM_BYTES = 64 * 1024 * 1024
VMEM_LIMIT = V7X_VMEM_BYTES - 8 * 1024 * 1024
BF16_SUBLANES = 16

F32_SUBLANES = 8
HG_LEVELS = 7
HG_MXU_LEVELS = (5,)


def _silu(t):
    return t * jax.nn.sigmoid(t)


def _dot(a, b):
    return jnp.dot(a, b, preferred_element_type=F32)


def _dot_nt(a, b):
    return lax.dot_general(a, b, (((1,), (1,)), ((), ())), preferred_element_type=F32)


def _dot_tn(a, b):
    return lax.dot_general(a, b, (((0,), (0,)), ((), ())), preferred_element_type=F32)


def _params(n_axes):
    return pltpu.CompilerParams(
        dimension_semantics=("arbitrary",) * n_axes, vmem_limit_bytes=VMEM_LIMIT)


def _mod_kernel(c_ref, w_ref, b_ref, o_ref):
    sc = _silu(c_ref[...])
    o_ref[...] = jnp.sum(sc * w_ref[...], axis=0, keepdims=True) + b_ref[...]


def _mod_call(c_col, w_ada, b_ada, layer):
    d, n = w_ada.shape[1], w_ada.shape[2]
    tn = 1024
    return pl.pallas_call(
        _mod_kernel,
        grid=(n // tn,),
        in_specs=[
            pl.BlockSpec((d, 1), lambda j: (0, 0)),
            pl.BlockSpec((None, d, tn), lambda j: (layer, 0, j)),
            pl.BlockSpec((None, 1, tn), lambda j: (layer, 0, j)),
        ],
        out_specs=pl.BlockSpec((1, tn), lambda j: (0, j)),
        out_shape=jax.ShapeDtypeStruct((1, n), F32),
        compiler_params=_params(1),
        name="mod",
    )(c_col, w_ada, b_ada.reshape(b_ada.shape[0], 1, n))


def _rope_kernel(pos_ref, f_ref, cos_ref, sin_ref):
    ang = pos_ref[...].astype(F32) * f_ref[...]
    cos_ref[...] = jnp.cos(ang)
    sin_ref[...] = jnp.sin(ang)


def _rope_call(pos_col, inv_freq):
    s = pos_col.shape[0]
    hd = inv_freq.shape[1]
    ts = min(s, 2048)
    return pl.pallas_call(
        _rope_kernel,
        grid=(s // ts,),
        in_specs=[pl.BlockSpec((ts, 1), lambda i: (i, 0)),
                  pl.BlockSpec((1, hd), lambda i: (0, 0))],
        out_specs=[pl.BlockSpec((ts, hd), lambda i: (i, 0))] * 2,
        out_shape=[jax.ShapeDtypeStruct((s, hd), F32)] * 2,
        compiler_params=_params(1),
        name="rope_table",
    )(pos_col, inv_freq)


def _norm_mod(x, g, scale, shift):
    y = x * lax.rsqrt(jnp.mean(x * x, axis=-1, keepdims=True) + NORM_EPS)
    return (y * g) * (1.0 + scale) + shift


def _in_kernel(x_ref, g_ref, sc_ref, sh_ref, w_ref, o_ref, h_even_ref, h_odd_ref, *, n_sub):
    i = pl.program_id(0)
    j = pl.program_id(1)
    rb = x_ref.shape[0]

    def norm_sub_block(h_ref):
        row0 = pl.multiple_of(jnp.minimum(j, n_sub - 1) * rb, rb)
        h_ref[pl.ds(row0, rb), :] = _norm_mod(
            x_ref[...], g_ref[...], sc_ref[...], sh_ref[...]).astype(BF16)

    @pl.when(i == 0)
    def _():
        norm_sub_block(h_even_ref)
        o_ref[...] = jnp.zeros_like(o_ref)

    @pl.when(i % 2 == 1)
    def _():
        norm_sub_block(h_odd_ref)
        o_ref[...] = _dot(h_even_ref[...], w_ref[...]).astype(o_ref.dtype)

    @pl.when((i > 0) & (i % 2 == 0))
    def _():
        norm_sub_block(h_even_ref)
        o_ref[...] = _dot(h_odd_ref[...], w_ref[...]).astype(o_ref.dtype)


IN_TN = 2048
IN_ROW_BLOCK = 256


def _in_call(x, g, scale, shift, w_tiles, layer):
    s, d = x.shape
    nj, tn = w_tiles.shape[1], w_tiles.shape[3]
    tm = min(s, 1024)
    rb = min(tm, IN_ROW_BLOCK)
    n_sub = tm // rb
    nt = s // tm
    assert n_sub <= nj
    vec = pl.BlockSpec((1, d), lambda i, j: (0, 0))

    def x_map(i, j):
        return (jnp.minimum(i, nt - 1) * n_sub + jnp.minimum(j, n_sub - 1), 0)

    return pl.pallas_call(
        functools.partial(_in_kernel, n_sub=n_sub),
        grid=(nt + 1, nj),
        in_specs=[
            pl.BlockSpec((rb, d), x_map),
            pl.BlockSpec((None, 1, d), lambda i, j: (layer, 0, 0)),
            vec, vec,
            pl.BlockSpec((None, None, d, tn), lambda i, j: (layer, j, 0, 0)),
        ],
        out_specs=pl.BlockSpec((tm, tn), lambda i, j: ((i + nt) % (nt + 1), j)),
        out_shape=jax.ShapeDtypeStruct((s + tm, nj * tn), BF16),
        scratch_shapes=[pltpu.VMEM((tm, d), BF16), pltpu.VMEM((tm, d), BF16)],
        compiler_params=_params(2),
        name="in_proj",
    )(x, g.reshape(g.shape[0], 1, d), scale, shift, w_tiles)


def _ret_tables():
    gamma = 1.0 - jnp.exp2(-5.0 - jnp.arange(RET_H, dtype=F32))
    log_g = jnp.log(gamma)
    idx = jnp.arange(CHUNK, dtype=F32)
    diff = idx[:, None] - idx[None, :]
    decay = jnp.where(diff[None] >= 0,
                      jnp.exp(jnp.maximum(diff, 0.0)[None] * log_g[:, None, None]), 0.0)
    xi = jnp.exp((idx[None, :] + 1.0) * log_g[:, None])[:, :, None]
    zeta = jnp.exp((CHUNK - 1.0 - idx[None, :]) * log_g[:, None])[:, :, None]
    g_chunk = jnp.exp(CHUNK * log_g)[:, None, None]
    return (decay, jnp.broadcast_to(xi, (RET_H, CHUNK, RET_DV)),
            jnp.broadcast_to(zeta, (RET_H, CHUNK, RET_DK)),
            jnp.broadcast_to(g_chunk, (RET_H, 1, RET_DV)))


def _hg_tables():
    t = np.arange(CHUNK)
    sums = [t[None, :] <= t[:, None]]
    upper, pair = [], []
    for l in range(HG_LEVELS):
        w = (CHUNK // 2) >> l
        up = (t // w) % 2 == 1
        upper.append(np.broadcast_to(up[:, None], (CHUNK, HG_DK)))
        pair.append(up[:, None] & ~up[None, :] & ((t[:, None] // (2 * w)) == (t[None, :] // (2 * w))))
        if l in HG_MXU_LEVELS:
            m = (t // (2 * w)) * (2 * w) + w - 1
            lo = np.minimum(t, m)[:, None]
            hi = np.maximum(t, m)[:, None]
            sums.append((t[None, :] > lo) & (t[None, :] <= hi))
    pair.append(t[:, None] == t[None, :])
    tsum = np.concatenate(sums, axis=0).astype(np.float32)
    return (jnp.asarray(tsum, BF16), jnp.asarray(np.stack(upper), F32),
            jnp.asarray(np.stack(pair), F32))


def _ret_head(h, p_ref, cos, sin, dec_ref, xi_ref, zeta_ref, gch_ref, r_ref, o_ref):
    half = RET_DK // 2

    def cols(part):
        c0 = (part * RET_H + h) * RET_DK
        return p_ref[:, c0:c0 + RET_DK]

    def rope(t):
        t1 = t[:, :half]
        t2 = t[:, half:]
        return jnp.concatenate([t1 * cos - t2 * sin, t2 * cos + t1 * sin], axis=-1)

    q = rope(cols(0).astype(F32))
    k = rope(cols(1).astype(F32)) * (RET_DK ** -0.5)
    v = cols(2)

    qb = q.astype(BF16)
    inner = _dot_nt(qb, k.astype(BF16)) * dec_ref[h]
    r = r_ref[h]
    o = _dot(inner.astype(BF16), v) + _dot(qb, r.astype(BF16)) * xi_ref[h]
    r_ref[h] = r * gch_ref[h] + _dot_tn((k * zeta_ref[h]).astype(BF16), v)

    oc = o - jnp.mean(o, axis=-1, keepdims=True)
    on = oc * lax.rsqrt(jnp.mean(oc * oc, axis=-1, keepdims=True) + HEAD_EPS)
    o_ref[:, h * RET_DV:(h + 1) * RET_DV] = (on * _silu(cols(3).astype(F32))).astype(o_ref.dtype)


def _hg_head(h, col0, p_ref, lb, tsum_ref, up_ref, pair_ref, st_ref, o_ref):
    def cols(part):
        c0 = col0 + (part * HG_H + h) * HG_DK
        return p_ref[:, c0:c0 + HG_DK]

    q = _silu(cols(0).astype(F32))
    f = lb + (1.0 - lb) * jax.nn.sigmoid(cols(1).astype(F32))
    k = 1.0 - f
    v = cols(2)

    l2 = jnp.log2(f)
    l_hi = l2.astype(BF16)
    l_lo = (l2 - l_hi.astype(F32)).astype(BF16)
    sums2 = _dot(tsum_ref[...], jnp.concatenate([l_hi, l_lo], axis=1))
    sums = sums2[:, :HG_DK] + sums2[:, HG_DK:]
    b = sums[0:CHUNK]

    def rows(t, r0, n):
        return t[r0:r0 + n]

    g = F32_SUBLANES
    diag = pair_ref[HG_LEVELS] * jnp.sum(q * k, axis=-1, keepdims=True)
    a_rows = [rows(diag, r, g) for r in range(0, CHUNK, g)]
    qk = q - k
    mxu_level = 0
    for l in range(HG_LEVELS):
        w = (CHUNK // 2) >> l
        if w >= g:
            xs, ups = [], []
            for r0 in range(0, CHUNK, 2 * w):
                b_mid = rows(b, r0 + w - 1, 1)
                xs.append(rows(k, r0, w) * jnp.exp2(b_mid - rows(b, r0, w)))
                xs.append(rows(q, r0 + w, w) * jnp.exp2(rows(b, r0 + w, w) - b_mid))
                ups.append(xs[-1])
            x = jnp.concatenate(xs, axis=0).astype(BF16)
            if w >= BF16_SUBLANES:
                z = _dot_nt(jnp.concatenate(ups, axis=0).astype(BF16), x)
                up_starts = [r0 + w for r0 in range(0, CHUNK, 2 * w)]
                for n, r0 in enumerate(up_starts):
                    for r in range(0, w, g):
                        a_rows[(r0 + r) // g] = (a_rows[(r0 + r) // g]
                                                 + pair_ref[l, r0 + r:r0 + r + g, :]
                                                 * rows(z, n * w + r, g))
                continue
        elif l in HG_MXU_LEVELS:
            mxu_level += 1
            x = ((k + up_ref[l] * qk) * jnp.exp2(rows(sums, mxu_level * CHUNK, CHUNK))).astype(BF16)
        elif 2 * w == g:
            b_mid = jnp.concatenate(
                [jnp.broadcast_to(rows(b, r0 + w - 1, 1), (2 * w, HG_DK))
                 for r0 in range(0, CHUNK, 2 * w)], axis=0)
            sign = 2.0 * up_ref[l] - 1.0
            x = ((k + up_ref[l] * qk) * jnp.exp2((b - b_mid) * sign)).astype(BF16)
        else:
            assert w == 1
            x = (k + up_ref[l] * (q * f - k)).astype(BF16)
        z = _dot_nt(x, x)
        for r in range(0, CHUNK, g):
            a_rows[r // g] = a_rows[r // g] + pair_ref[l, r:r + g, :] * rows(z, r, g)
    a = jnp.concatenate(a_rows, axis=0)

    e_b = jnp.exp2(b)
    e_last = jnp.exp2(rows(b, CHUNK - 1, 1) - b)
    st = st_ref[h]
    o = _dot(a.astype(BF16), v) + _dot_nt((q * e_b).astype(BF16), st.astype(BF16))
    st_ref[h] = st * rows(e_b, CHUNK - 1, 1) + _dot_tn(v, (k * e_last).astype(BF16))

    on = o * lax.rsqrt(jnp.mean(o * o, axis=-1, keepdims=True) + HEAD_EPS)
    o_ref[:, h * HG_DV:(h + 1) * HG_DV] = (on * _silu(cols(3).astype(F32))).astype(o_ref.dtype)


def _mix_kernel(layer, hg_col0, p_ref, cos_ref, sin_ref, dec_ref, xi_ref, zeta_ref, gch_ref,
                lbp_ref, tsum_ref, up_ref, pair_ref, oret_ref, ohg_ref, r_ref, st_ref):
    @pl.when(pl.program_id(0) == 0)
    def _():
        r_ref[...] = jnp.zeros_like(r_ref)
        st_ref[...] = jnp.zeros_like(st_ref)

    lbp = lbp_ref[...]
    e = jnp.exp(lbp - jnp.max(lbp, axis=0, keepdims=True))
    lb_all = jnp.sum(e[:layer + 1], axis=0, keepdims=True) / jnp.sum(e, axis=0, keepdims=True)

    cos = cos_ref[...]
    sin = sin_ref[...]
    for h in range(RET_H):
        _ret_head(h, p_ref, cos, sin, dec_ref, xi_ref, zeta_ref, gch_ref, r_ref, oret_ref)
    for h in range(HG_H):
        _hg_head(h, hg_col0, p_ref, lb_all[:, h * HG_DK:(h + 1) * HG_DK],
                 tsum_ref, up_ref, pair_ref, st_ref, ohg_ref)


def _mix_call(p, cos, sin, hg_lb, layer):
    s = cos.shape[0]
    nc = s // CHUNK
    ret_cols = 4 * RET_H * RET_DK
    mix_cols = ret_cols + 4 * HG_H * HG_DK
    tables = _ret_tables() + (hg_lb,) + _hg_tables()

    def const(arr):
        return pl.BlockSpec(arr.shape, lambda c: (0,) * arr.ndim)

    tab = pl.BlockSpec((CHUNK, RET_DK // 2), lambda c: (c, 0))
    return pl.pallas_call(
        functools.partial(_mix_kernel, layer, ret_cols),
        grid=(nc,),
        in_specs=[pl.BlockSpec((CHUNK, mix_cols), lambda c: (c, 0)), tab, tab]
        + [const(t) for t in tables],
        out_specs=[pl.BlockSpec((CHUNK, RET_H * RET_DV), lambda c: (c, 0)),
                   pl.BlockSpec((CHUNK, HG_H * HG_DV), lambda c: (c, 0))],
        out_shape=[jax.ShapeDtypeStruct((s, RET_H * RET_DV), BF16),
                   jax.ShapeDtypeStruct((s, HG_H * HG_DV), BF16)],
        scratch_shapes=[pltpu.VMEM((RET_H, RET_DK, RET_DV), F32),
                        pltpu.VMEM((HG_H, HG_DV, HG_DK), F32)],
        compiler_params=_params(1),
        name="mixers",
    )(p, cos, sin, *tables)


def _merge_kernel(a_ref, b_ref, ga_ref, gb_ref, x_ref, gate_ref, wa_ref, wb_ref, wo_ref, o_ref):
    ya = _dot(a_ref[...], wa_ref[...])
    yb = _dot(b_ref[...], wb_ref[...])
    merged = (jax.nn.sigmoid(ga_ref[...].astype(F32)) * ya
              + jax.nn.sigmoid(gb_ref[...].astype(F32)) * yb)
    o_ref[...] = x_ref[...] + gate_ref[...] * _dot(merged.astype(BF16), wo_ref[...])


def _merge_call(oa, ob, p, x, gate, w_ret_o, w_hg_o, w_out, layer, gate_col_block):
    s, d = x.shape
    tm = min(s, 256)

    def full(w):
        return pl.BlockSpec((None,) + w.shape[1:], lambda i: (layer, 0, 0))

    return pl.pallas_call(
        _merge_kernel,
        grid=(s // tm,),
        in_specs=[
            pl.BlockSpec((tm, oa.shape[1]), lambda i: (i, 0)),
            pl.BlockSpec((tm, ob.shape[1]), lambda i: (i, 0)),
            pl.BlockSpec((tm, d), lambda i: (i, gate_col_block)),
            pl.BlockSpec((tm, d), lambda i: (i, gate_col_block + 1)),
            pl.BlockSpec((tm, d), lambda i: (i, 0)),
            pl.BlockSpec((1, d), lambda i: (0, 0)),
            full(w_ret_o), full(w_hg_o), full(w_out),
        ],
        out_specs=pl.BlockSpec((tm, d), lambda i: (i, 0)),
        out_shape=jax.ShapeDtypeStruct((s, d), F32),
        compiler_params=_params(1),
        name="merge",
    )(oa, ob, p, p, x, gate, w_ret_o, w_hg_o, w_out)


def _ffn_kernel(x_ref, g_ref, sc_ref, sh_ref, gate_ref, gf_ref,
                wa_ref, wb_ref, cwa_ref, cwb_ref, cba_ref, cbb_ref, wd_ref,
                o_ref, hx_ref, *, final_norm):
    i = pl.program_id(0)
    j = pl.program_id(1)
    halo = BF16_SUBLANES
    tm = x_ref.shape[0]

    @pl.when(j == 0)
    def _():
        @pl.when(i == 0)
        def _():
            hx_ref[0:halo, :] = jnp.zeros((halo, hx_ref.shape[1]), BF16)

        @pl.when(i > 0)
        def _():
            hx_ref[0:halo, :] = hx_ref[tm:tm + halo, :]

        hx_ref[halo:, :] = _norm_mod(
            x_ref[...], g_ref[...], sc_ref[...], sh_ref[...]).astype(BF16)
        o_ref[...] = jnp.zeros_like(o_ref)

    hx = hx_ref[...]

    def conv(u, cw_ref, cb_ref):
        cw = cw_ref[...]
        y = (cw[0:1] * pltpu.roll(u, 2, 0) + cw[1:2] * pltpu.roll(u, 1, 0)
             + cw[2:3] * u + cb_ref[...])
        return y[halo:, :]

    a = conv(_dot(hx, wa_ref[...]), cwa_ref, cba_ref)
    b = conv(_dot(hx, wb_ref[...]), cwb_ref, cbb_ref)
    o_ref[...] += _dot((_silu(a) * b).astype(BF16), wd_ref[...])

    @pl.when(j == pl.num_programs(1) - 1)
    def _():
        x2 = x_ref[...] + gate_ref[...] * o_ref[...]
        if final_norm:
            x2 = (x2 * lax.rsqrt(jnp.mean(x2 * x2, axis=-1, keepdims=True) + NORM_EPS)
                  * gf_ref[...])
        o_ref[...] = x2


FFN_TF = 512


def _ffn_call(x, g, scale, shift, gate, g_final, w_up_tiles, conv_w, conv_b, w_down, layer,
              final_norm):
    s, d = x.shape
    dff = w_down.shape[1]
    tm = min(s, 1024)
    tf = FFN_TF
    nj = dff // tf
    assert CONV_W - 1 <= BF16_SUBLANES
    vec = pl.BlockSpec((1, d), lambda i, j: (0, 0))
    x_spec = pl.BlockSpec((tm, d), lambda i, j: (i, 0), pipeline_mode=pl.Buffered(1))

    def cols(shape, off):
        return pl.BlockSpec((None,) + shape, lambda i, j: (layer, 0, off + j))

    def up_tile(half):
        return pl.BlockSpec((None, None, None, d, tf), lambda i, j: (layer, half, j, 0, 0))

    conv_b3 = conv_b.reshape(conv_b.shape[0], 1, 2 * dff)
    return pl.pallas_call(
        functools.partial(_ffn_kernel, final_norm=final_norm),
        grid=(s // tm, nj),
        in_specs=[
            x_spec,
            pl.BlockSpec((None, 1, d), lambda i, j: (layer, 0, 0)),
            vec, vec, vec, vec,
            up_tile(0), up_tile(1),
            cols((CONV_W, tf), 0), cols((CONV_W, tf), nj),
            cols((1, tf), 0), cols((1, tf), nj),
            pl.BlockSpec((None, tf, d), lambda i, j: (layer, j, 0)),
        ],
        out_specs=pl.BlockSpec((tm, d), lambda i, j: (i, 0)),
        out_shape=jax.ShapeDtypeStruct((s, d), F32),
        scratch_shapes=[pltpu.VMEM((tm + BF16_SUBLANES, d), BF16)],
        compiler_params=_params(2),
        name="conv_ffn",
    )(x, g.reshape(g.shape[0], 1, d), scale, shift, gate, g_final.reshape(1, d),
      w_up_tiles, w_up_tiles, conv_w, conv_w, conv_b3, conv_b3, w_down)


def kernel(x, c, positions, w_ada, b_ada, g_norm1, w_in, w_ret_o, w_hg_o, w_out, hg_lb,
           g_norm2, w_up, conv_w, conv_b, w_down, g_final):
    batch, s, d = x.shape
    depth = w_in.shape[0]
    assert batch == 1 and s % CHUNK == 0
    ret_w = RET_H * RET_DK
    gate_col_block = (4 * ret_w + 4 * HG_H * HG_DK) // d

    xs = x.reshape(s, d)
    inv_freq = ROPE_BASE ** (-jnp.arange(0, RET_DK, 2, dtype=F32) / RET_DK)
    cos, sin = _rope_call(positions.reshape(s, 1), inv_freq.reshape(1, RET_DK // 2))

    n_in = w_in.shape[2]
    w_in_t = w_in.astype(BF16).reshape(depth, d, n_in // IN_TN, IN_TN).transpose(0, 2, 1, 3)
    w_ret_o_b = w_ret_o.astype(BF16)
    w_hg_o_b = w_hg_o.astype(BF16)
    w_out_b = w_out.astype(BF16)
    dff = w_down.shape[1]
    w_up_t = w_up.astype(BF16).reshape(depth, d, 2, dff // FFN_TF, FFN_TF).transpose(0, 2, 3, 1, 4)
    w_down_b = w_down.astype(BF16)

    for l in range(depth):
        mod = _mod_call(c.reshape(d, 1), w_ada, b_ada, l)
        shift1, scale1, gate1, shift2, scale2, gate2 = [
            mod[:, n * d:(n + 1) * d] for n in range(6)]
        p = _in_call(xs, g_norm1, scale1, shift1, w_in_t, l)
        o_ret, o_hg = _mix_call(p, cos, sin, hg_lb, l)
        xs = _merge_call(o_ret, o_hg, p, xs, gate1, w_ret_o_b, w_hg_o_b, w_out_b, l,
                         gate_col_block)
        xs = _ffn_call(xs, g_norm2, scale2, shift2, gate2, g_final, w_up_t, conv_w, conv_b,
                       w_down_b, l, final_norm=(l == depth - 1))
    return xs.reshape(batch, s, d)
```

```python
import functools

import numpy as np
import jax
import jax.numpy as jnp
from jax import lax
from jax.experimental import pallas as pl
from jax.experimental.pallas import tpu as pltpu

F32 = jnp.float32
BF16 = jnp.bfloat16

RET_H = 4
RET_DK = 256
RET_DV = 256
HG_H = 8
HG_DK = 128
HG_DV = 128
CHUNK = 128
CONV_W = 3
ROPE_BASE = 10000.0
NORM_EPS = 1e-6
HEAD_EPS = 1e-5

V7X_VMEM_BYTES = 64 * 1024 * 1024
VMEM_LIMIT = V7X_VMEM_BYTES - 8 * 1024 * 1024
BF16_SUBLANES = 16

F32_SUBLANES = 8
HG_LEVELS = 7
HG_MXU_LEVELS = (5,)


def _silu(t):
    return t * jax.nn.sigmoid(t)


def _dot(a, b):
    return jnp.dot(a, b, preferred_element_type=F32)


def _dot_nt(a, b):
    return lax.dot_general(a, b, (((1,), (1,)), ((), ())), preferred_element_type=F32)


def _dot_tn(a, b):
    return lax.dot_general(a, b, (((0,), (0,)), ((), ())), preferred_element_type=F32)


def _params(n_axes):
    return pltpu.CompilerParams(
        dimension_semantics=("arbitrary",) * n_axes, vmem_limit_bytes=VMEM_LIMIT)


def _mod_kernel(c_ref, w_ref, b_ref, o_ref):
    sc = _silu(c_ref[...])
    o_ref[...] = jnp.sum(sc * w_ref[...], axis=0, keepdims=True) + b_ref[...]


def _mod_call(c_col, w_ada, b_ada, layer):
    d, n = w_ada.shape[1], w_ada.shape[2]
    tn = 1024
    return pl.pallas_call(
        _mod_kernel,
        grid=(n // tn,),
        in_specs=[
            pl.BlockSpec((d, 1), lambda j: (0, 0)),
            pl.BlockSpec((None, d, tn), lambda j: (layer, 0, j)),
            pl.BlockSpec((None, 1, tn), lambda j: (layer, 0, j)),
        ],
        out_specs=pl.BlockSpec((1, tn), lambda j: (0, j)),
        out_shape=jax.ShapeDtypeStruct((1, n), F32),
        compiler_params=_params(1),
        name="mod",
    )(c_col, w_ada, b_ada.reshape(b_ada.shape[0], 1, n))


def _rope_kernel(pos_ref, f_ref, cos_ref, sin_ref):
    ang = pos_ref[...].astype(F32) * f_ref[...]
    cos_ref[...] = jnp.cos(ang)
    sin_ref[...] = jnp.sin(ang)


def _rope_call(pos_col, inv_freq):
    s = pos_col.shape[0]
    hd = inv_freq.shape[1]
    ts = min(s, 2048)
    return pl.pallas_call(
        _rope_kernel,
        grid=(s // ts,),
        in_specs=[pl.BlockSpec((ts, 1), lambda i: (i, 0)),
                  pl.BlockSpec((1, hd), lambda i: (0, 0))],
        out_specs=[pl.BlockSpec((ts, hd), lambda i: (i, 0))] * 2,
        out_shape=[jax.ShapeDtypeStruct((s, hd), F32)] * 2,
        compiler_params=_params(1),
        name="rope_table",
    )(pos_col, inv_freq)


def _norm_mod(x, g, scale, shift):
    y = x * lax.rsqrt(jnp.mean(x * x, axis=-1, keepdims=True) + NORM_EPS)
    return (y * g) * (1.0 + scale) + shift


def _in_kernel(x_ref, g_ref, sc_ref, sh_ref, w_ref, o_ref, h_even_ref, h_odd_ref, *, n_sub):
    i = pl.program_id(0)
    j = pl.program_id(1)
    rb = x_ref.shape[0]

    def norm_sub_block(h_ref):
        row0 = pl.multiple_of(jnp.minimum(j, n_sub - 1) * rb, rb)
        h_ref[pl.ds(row0, rb), :] = _norm_mod(
            x_ref[...], g_ref[...], sc_ref[...], sh_ref[...]).astype(BF16)

    @pl.when(i == 0)
    def _():
        norm_sub_block(h_even_ref)
        o_ref[...] = jnp.zeros_like(o_ref)

    @pl.when(i % 2 == 1)
    def _():
        norm_sub_block(h_odd_ref)
        o_ref[...] = _dot(h_even_ref[...], w_ref[...]).astype(o_ref.dtype)

    @pl.when((i > 0) & (i % 2 == 0))
    def _():
        norm_sub_block(h_even_ref)
        o_ref[...] = _dot(h_odd_ref[...], w_ref[...]).astype(o_ref.dtype)


IN_TN = 2048
IN_ROW_BLOCK = 256


def _in_call(x, g, scale, shift, w, layer):
    s, d = x.shape
    tn = IN_TN
    nj = w.shape[2] // tn
    tm = min(s, 1024)
    rb = min(tm, IN_ROW_BLOCK)
    n_sub = tm // rb
    nt = s // tm
    assert n_sub <= nj
    vec = pl.BlockSpec((1, d), lambda i, j: (0, 0))

    def x_map(i, j):
        return (jnp.minimum(i, nt - 1) * n_sub + jnp.minimum(j, n_sub - 1), 0)

    return pl.pallas_call(
        functools.partial(_in_kernel, n_sub=n_sub),
        grid=(nt + 1, nj),
        in_specs=[
            pl.BlockSpec((rb, d), x_map),
            pl.BlockSpec((None, 1, d), lambda i, j: (layer, 0, 0)),
            vec, vec,
            pl.BlockSpec((None, d, tn), lambda i, j: (layer, 0, j)),
        ],
        out_specs=pl.BlockSpec((tm, tn), lambda i, j: ((i + nt) % (nt + 1), j)),
        out_shape=jax.ShapeDtypeStruct((s + tm, nj * tn), BF16),
        scratch_shapes=[pltpu.VMEM((tm, d), BF16), pltpu.VMEM((tm, d), BF16)],
        compiler_params=_params(2),
        name="in_proj",
    )(x, g.reshape(g.shape[0], 1, d), scale, shift, w)


def _ret_tables():
    gamma = 1.0 - jnp.exp2(-5.0 - jnp.arange(RET_H, dtype=F32))
    log_g = jnp.log(gamma)
    idx = jnp.arange(CHUNK, dtype=F32)
    diff = idx[:, None] - idx[None, :]
    decay = jnp.where(diff[None] >= 0,
                      jnp.exp(jnp.maximum(diff, 0.0)[None] * log_g[:, None, None]), 0.0)
    xi = jnp.exp((idx[None, :] + 1.0) * log_g[:, None])[:, :, None]
    zeta = jnp.exp((CHUNK - 1.0 - idx[None, :]) * log_g[:, None])[:, :, None]
    g_chunk = jnp.exp(CHUNK * log_g)[:, None, None]
    return (decay, jnp.broadcast_to(xi, (RET_H, CHUNK, RET_DV)),
            jnp.broadcast_to(zeta, (RET_H, CHUNK, RET_DK)),
            jnp.broadcast_to(g_chunk, (RET_H, 1, RET_DV)))


def _hg_tables():
    t = np.arange(CHUNK)
    sums = [t[None, :] <= t[:, None]]
    upper, pair = [], []
    for l in range(HG_LEVELS):
        w = (CHUNK // 2) >> l
        up = (t // w) % 2 == 1
        upper.append(np.broadcast_to(up[:, None], (CHUNK, HG_DK)))
        pair.append(up[:, None] & ~up[None, :] & ((t[:, None] // (2 * w)) == (t[None, :] // (2 * w))))
        if l in HG_MXU_LEVELS:
            m = (t // (2 * w)) * (2 * w) + w - 1
            lo = np.minimum(t, m)[:, None]
            hi = np.maximum(t, m)[:, None]
            sums.append((t[None, :] > lo) & (t[None, :] <= hi))
    pair.append(t[:, None] == t[None, :])
    tsum = np.concatenate(sums, axis=0).astype(np.float32)
    return (jnp.asarray(tsum, BF16), jnp.asarray(np.stack(upper), F32),
            jnp.asarray(np.stack(pair), F32))


def _ret_head(h, p_ref, cos, sin, dec_ref, xi_ref, zeta_ref, gch_ref, r_ref, o_ref):
    half = RET_DK // 2

    def cols(part):
        c0 = (part * RET_H + h) * RET_DK
        return p_ref[:, c0:c0 + RET_DK]

    def rope(t):
        t1 = t[:, :half]
        t2 = t[:, half:]
        return jnp.concatenate([t1 * cos - t2 * sin, t2 * cos + t1 * sin], axis=-1)

    q = rope(cols(0).astype(F32))
    k = rope(cols(1).astype(F32)) * (RET_DK ** -0.5)
    v = cols(2)

    qb = q.astype(BF16)
    inner = _dot_nt(qb, k.astype(BF16)) * dec_ref[h]
    r = r_ref[h]
    o = _dot(inner.astype(BF16), v) + _dot(qb, r.astype(BF16)) * xi_ref[h]
    r_ref[h] = r * gch_ref[h] + _dot_tn((k * zeta_ref[h]).astype(BF16), v)

    oc = o - jnp.mean(o, axis=-1, keepdims=True)
    on = oc * lax.rsqrt(jnp.mean(oc * oc, axis=-1, keepdims=True) + HEAD_EPS)
    o_ref[:, h * RET_DV:(h + 1) * RET_DV] = (on * _silu(cols(3).astype(F32))).astype(o_ref.dtype)


def _hg_head(h, col0, p_ref, lb, tsum_ref, up_ref, pair_ref, st_ref, o_ref):
    def cols(part):
        c0 = col0 + (part * HG_H + h) * HG_DK
        return p_ref[:, c0:c0 + HG_DK]

    q = _silu(cols(0).astype(F32))
    f = lb + (1.0 - lb) * jax.nn.sigmoid(cols(1).astype(F32))
    k = 1.0 - f
    v = cols(2)

    l2 = jnp.log2(f)
    l_hi = l2.astype(BF16)
    l_lo = (l2 - l_hi.astype(F32)).astype(BF16)
    sums2 = _dot(tsum_ref[...], jnp.concatenate([l_hi, l_lo], axis=1))
    sums = sums2[:, :HG_DK] + sums2[:, HG_DK:]
    b = sums[0:CHUNK]

    def rows(t, r0, n):
        return t[r0:r0 + n]

    g = F32_SUBLANES
    diag = pair_ref[HG_LEVELS] * jnp.sum(q * k, axis=-1, keepdims=True)
    a_rows = [rows(diag, r, g) for r in range(0, CHUNK, g)]
    qk = q - k
    mxu_level = 0
    for l in range(HG_LEVELS):
        w = (CHUNK // 2) >> l
        if w >= g:
            xs, ups = [], []
            for r0 in range(0, CHUNK, 2 * w):
                b_mid = rows(b, r0 + w - 1, 1)
                xs.append(rows(k, r0, w) * jnp.exp2(b_mid - rows(b, r0, w)))
                xs.append(rows(q, r0 + w, w) * jnp.exp2(rows(b, r0 + w, w) - b_mid))
                ups.append(xs[-1])
            x = jnp.concatenate(xs, axis=0).astype(BF16)
            if w >= BF16_SUBLANES:
                z = _dot_nt(jnp.concatenate(ups, axis=0).astype(BF16), x)
                up_starts = [r0 + w for r0 in range(0, CHUNK, 2 * w)]
                for n, r0 in enumerate(up_starts):
                    for r in range(0, w, g):
                        a_rows[(r0 + r) // g] = (a_rows[(r0 + r) // g]
                                                 + pair_ref[l, r0 + r:r0 + r + g, :]
                                                 * rows(z, n * w + r, g))
                continue
        elif l in HG_MXU_LEVELS:
            mxu_level += 1
            x = ((k + up_ref[l] * qk) * jnp.exp2(rows(sums, mxu_level * CHUNK, CHUNK))).astype(BF16)
        elif 2 * w == g:
            b_mid = jnp.concatenate(
                [jnp.broadcast_to(rows(b, r0 + w - 1, 1), (2 * w, HG_DK))
                 for r0 in range(0, CHUNK, 2 * w)], axis=0)
            sign = 2.0 * up_ref[l] - 1.0
            x = ((k + up_ref[l] * qk) * jnp.exp2((b - b_mid) * sign)).astype(BF16)
        else:
            assert w == 1
            x = (k + up_ref[l] * (q * f - k)).astype(BF16)
        z = _dot_nt(x, x)
        for r in range(0, CHUNK, g):
            a_rows[r // g] = a_rows[r // g] + pair_ref[l, r:r + g, :] * rows(z, r, g)
    a = jnp.concatenate(a_rows, axis=0)

    e_b = jnp.exp2(b)
    e_last = jnp.exp2(rows(b, CHUNK - 1, 1) - b)
    st = st_ref[h]
    o = _dot(a.astype(BF16), v) + _dot_nt((q * e_b).astype(BF16), st.astype(BF16))
    st_ref[h] = st * rows(e_b, CHUNK - 1, 1) + _dot_tn(v, (k * e_last).astype(BF16))

    on = o * lax.rsqrt(jnp.mean(o * o, axis=-1, keepdims=True) + HEAD_EPS)
    o_ref[:, h * HG_DV:(h + 1) * HG_DV] = (on * _silu(cols(3).astype(F32))).astype(o_ref.dtype)


def _mix_kernel(layer, hg_col0, p_ref, cos_ref, sin_ref, dec_ref, xi_ref, zeta_ref, gch_ref,
                lbp_ref, tsum_ref, up_ref, pair_ref, oret_ref, ohg_ref, r_ref, st_ref):
    @pl.when(pl.program_id(0) == 0)
    def _():
        r_ref[...] = jnp.zeros_like(r_ref)
        st_ref[...] = jnp.zeros_like(st_ref)

    lbp = lbp_ref[...]
    e = jnp.exp(lbp - jnp.max(lbp, axis=0, keepdims=True))
    lb_all = jnp.sum(e[:layer + 1], axis=0, keepdims=True) / jnp.sum(e, axis=0, keepdims=True)

    cos = cos_ref[...]
    sin = sin_ref[...]
    for h in range(RET_H):
        _ret_head(h, p_ref, cos, sin, dec_ref, xi_ref, zeta_ref, gch_ref, r_ref, oret_ref)
    for h in range(HG_H):
        _hg_head(h, hg_col0, p_ref, lb_all[:, h * HG_DK:(h + 1) * HG_DK],
                 tsum_ref, up_ref, pair_ref, st_ref, ohg_ref)


def _mix_call(p, cos, sin, hg_lb, layer):
    s = cos.shape[0]
    nc = s // CHUNK
    ret_cols = 4 * RET_H * RET_DK
    mix_cols = ret_cols + 4 * HG_H * HG_DK
    tables = _ret_tables() + (hg_lb,) + _hg_tables()

    def const(arr):
        return pl.BlockSpec(arr.shape, lambda c: (0,) * arr.ndim)

    tab = pl.BlockSpec((CHUNK, RET_DK // 2), lambda c: (c, 0))
    return pl.pallas_call(
        functools.partial(_mix_kernel, layer, ret_cols),
        grid=(nc,),
        in_specs=[pl.BlockSpec((CHUNK, mix_cols), lambda c: (c, 0)), tab, tab]
        + [const(t) for t in tables],
        out_specs=[pl.BlockSpec((CHUNK, RET_H * RET_DV), lambda c: (c, 0)),
                   pl.BlockSpec((CHUNK, HG_H * HG_DV), lambda c: (c, 0))],
        out_shape=[jax.ShapeDtypeStruct((s, RET_H * RET_DV), BF16),
                   jax.ShapeDtypeStruct((s, HG_H * HG_DV), BF16)],
        scratch_shapes=[pltpu.VMEM((RET_H, RET_DK, RET_DV), F32),
                        pltpu.VMEM((HG_H, HG_DV, HG_DK), F32)],
        compiler_params=_params(1),
        name="mixers",
    )(p, cos, sin, *tables)


def _merge_kernel(a_ref, b_ref, ga_ref, gb_ref, x_ref, gate_ref, g2_ref, sc2_ref, sh2_ref,
                  wa_ref, wb_ref, wo_ref, o_ref, h_ref):
    ya = _dot(a_ref[...], wa_ref[...])
    yb = _dot(b_ref[...], wb_ref[...])
    merged = (jax.nn.sigmoid(ga_ref[...].astype(F32)) * ya
              + jax.nn.sigmoid(gb_ref[...].astype(F32)) * yb)
    x1 = x_ref[...] + gate_ref[...] * _dot(merged.astype(BF16), wo_ref[...])
    o_ref[...] = x1
    h_ref[...] = _norm_mod(x1, g2_ref[...], sc2_ref[...], sh2_ref[...]).astype(BF16)


def _merge_call(oa, ob, p, x, gate, g2, scale2, shift2, w_ret_o, w_hg_o, w_out, layer,
                gate_col_block):
    s, d = x.shape
    tm = min(s, 256)
    vec = pl.BlockSpec((1, d), lambda i: (0, 0))

    def full(w):
        return pl.BlockSpec((None,) + w.shape[1:], lambda i: (layer, 0, 0))

    return pl.pallas_call(
        _merge_kernel,
        grid=(s // tm,),
        in_specs=[
            pl.BlockSpec((tm, oa.shape[1]), lambda i: (i, 0)),
            pl.BlockSpec((tm, ob.shape[1]), lambda i: (i, 0)),
            pl.BlockSpec((tm, d), lambda i: (i, gate_col_block)),
            pl.BlockSpec((tm, d), lambda i: (i, gate_col_block + 1)),
            pl.BlockSpec((tm, d), lambda i: (i, 0)),
            vec,
            pl.BlockSpec((None, 1, d), lambda i: (layer, 0, 0)),
            vec, vec,
            full(w_ret_o), full(w_hg_o), full(w_out),
        ],
        out_specs=[pl.BlockSpec((tm, d), lambda i: (i, 0))] * 2,
        out_shape=[jax.ShapeDtypeStruct((s, d), F32), jax.ShapeDtypeStruct((s, d), BF16)],
        compiler_params=_params(1),
        name="merge",
    )(oa, ob, p, p, x, gate, g2.reshape(g2.shape[0], 1, d), scale2, shift2,
      w_ret_o, w_hg_o, w_out)


FFN_OUT_ROWS = 256


def _ffn_kernel(h_hbm, x_ref, gate_ref, gf_ref, wa_ref, wb_ref, cwa_ref, cwb_ref, cba_ref,
                cbb_ref, wd_ref, o_ref, hx_ref, acc_ref, hsem, *, nj, final_norm):
    i = pl.program_id(0)
    j = pl.program_id(1)
    nt = pl.num_programs(0)
    halo = BF16_SUBLANES
    tm = acc_ref.shape[0]
    slot = i % 2

    def first_tile_copy():
        return pltpu.make_async_copy(h_hbm.at[pl.ds(0, tm), :],
                                     hx_ref.at[0, pl.ds(halo, tm), :], hsem.at[0])

    def tile_copy(t, s):
        row0 = pl.multiple_of(t * tm - halo, halo)
        return pltpu.make_async_copy(h_hbm.at[pl.ds(row0, tm + halo), :], hx_ref.at[s],
                                     hsem.at[s])

    @pl.when(j == 0)
    def _():
        @pl.when(i == 0)
        def _():
            hx_ref[0, 0:halo, :] = jnp.zeros((halo, hx_ref.shape[2]), BF16)
            first_tile_copy().start()
            first_tile_copy().wait()

        @pl.when(i > 0)
        def _():
            tile_copy(i, slot).wait()

        @pl.when(i + 1 < nt)
        def _():
            tile_copy(i + 1, 1 - slot).start()

        acc_ref[...] = jnp.zeros_like(acc_ref)

    @pl.when(j < nj)
    def _():
        hx = hx_ref[slot]

        def conv(u, cw_ref, cb_ref):
            cw = cw_ref[...]
            y = (cw[0:1] * pltpu.roll(u, 2, 0) + cw[1:2] * pltpu.roll(u, 1, 0)
                 + cw[2:3] * u + cb_ref[...])
            return y[halo:, :]

        a = conv(_dot(hx, wa_ref[...]), cwa_ref, cba_ref)
        b = conv(_dot(hx, wb_ref[...]), cwb_ref, cbb_ref)
        acc_ref[...] += _dot((_silu(a) * b).astype(BF16), wd_ref[...])

    @pl.when(j >= nj)
    def _():
        rows = o_ref.shape[0]
        row0 = pl.multiple_of((j - nj) * rows, rows)
        x2 = x_ref[...] + gate_ref[...] * acc_ref[pl.ds(row0, rows), :]
        if final_norm:
            x2 = (x2 * lax.rsqrt(jnp.mean(x2 * x2, axis=-1, keepdims=True) + NORM_EPS)
                  * gf_ref[...])
        o_ref[...] = x2


def _ffn_call(x, h, gate, g_final, w_up, conv_w, conv_b, w_down, layer, final_norm):
    s, d = x.shape
    dff = w_down.shape[1]
    tm = min(s, 1024)
    tf = 512
    nj = dff // tf
    ro = min(tm, FFN_OUT_ROWS)
    n_out = tm // ro
    halo = BF16_SUBLANES
    assert CONV_W - 1 <= halo
    vec = pl.BlockSpec((1, d), lambda i, j: (0, 0))

    def cols(shape, off):
        return pl.BlockSpec((None,) + shape,
                            lambda i, j: (layer, 0, off + jnp.minimum(j, nj - 1)))

    def out_rows(i, j):
        return (jnp.maximum(i * n_out + jnp.maximum(j - nj, -1), 0), 0)

    conv_b3 = conv_b.reshape(conv_b.shape[0], 1, 2 * dff)
    return pl.pallas_call(
        functools.partial(_ffn_kernel, nj=nj, final_norm=final_norm),
        grid=(s // tm, nj + n_out),
        in_specs=[
            pl.BlockSpec(memory_space=pl.ANY),
            pl.BlockSpec((ro, d), out_rows),
            vec, vec,
            cols((d, tf), 0), cols((d, tf), nj),
            cols((CONV_W, tf), 0), cols((CONV_W, tf), nj),
            cols((1, tf), 0), cols((1, tf), nj),
            pl.BlockSpec((None, tf, d), lambda i, j: (layer, jnp.minimum(j, nj - 1), 0)),
        ],
        out_specs=pl.BlockSpec((ro, d), out_rows),
        out_shape=jax.ShapeDtypeStruct((s, d), F32),
        scratch_shapes=[pltpu.VMEM((2, tm + halo, d), BF16), pltpu.VMEM((tm, d), F32),
                        pltpu.SemaphoreType.DMA((2,))],
        compiler_params=_params(2),
        name="conv_ffn",
    )(h, x, gate, g_final.reshape(1, d), w_up, w_up, conv_w, conv_w, conv_b3, conv_b3, w_down)


def kernel(x, c, positions, w_ada, b_ada, g_norm1, w_in, w_ret_o, w_hg_o, w_out, hg_lb,
           g_norm2, w_up, conv_w, conv_b, w_down, g_final):
    batch, s, d = x.shape
    depth = w_in.shape[0]
    assert batch == 1 and s % CHUNK == 0
    ret_w = RET_H * RET_DK
    gate_col_block = (4 * ret_w + 4 * HG_H * HG_DK) // d

    xs = x.reshape(s, d)
    inv_freq = ROPE_BASE ** (-jnp.arange(0, RET_DK, 2, dtype=F32) / RET_DK)
    cos, sin = _rope_call(positions.reshape(s, 1), inv_freq.reshape(1, RET_DK // 2))

    w_in_b = w_in.astype(BF16)
    w_ret_o_b = w_ret_o.astype(BF16)
    w_hg_o_b = w_hg_o.astype(BF16)
    w_out_b = w_out.astype(BF16)
    w_up_b = w_up.astype(BF16)
    w_down_b = w_down.astype(BF16)

    for l in range(depth):
        mod = _mod_call(c.reshape(d, 1), w_ada, b_ada, l)
        shift1, scale1, gate1, shift2, scale2, gate2 = [
            mod[:, n * d:(n + 1) * d] for n in range(6)]
        p = _in_call(xs, g_norm1, scale1, shift1, w_in_b, l)
        o_ret, o_hg = _mix_call(p, cos, sin, hg_lb, l)
        xs, h2 = _merge_call(o_ret, o_hg, p, xs, gate1, g_norm2, scale2, shift2,
                             w_ret_o_b, w_hg_o_b, w_out_b, l, gate_col_block)
        xs = _ffn_call(xs, h2, gate2, g_final, w_up_b, conv_w, conv_b, w_down_b, l,
                       final_norm=(l == depth - 1))
    return xs.reshape(batch, s, d)
```

```python
import functools

import numpy as np
import jax
import jax.numpy as jnp
from jax import lax
from jax.experimental import pallas as pl
from jax.experimental.pallas import tpu as pltpu

F32 = jnp.float32
BF16 = jnp.bfloat16

RET_H = 4
RET_DK = 256
RET_DV = 256
HG_H = 8
HG_DK = 128
HG_DV = 128
CHUNK = 128
CONV_W = 3
ROPE_BASE = 10000.0
NORM_EPS = 1e-6
HEAD_EPS = 1e-5

V7X_VMEM_BYTES = 64 * 1024 * 1024
VMEM_LIMIT = V7X_VMEM_BYTES - 6 * 1024 * 1024
BF16_SUBLANES = 16

F32_SUBLANES = 8
HG_LEVELS = 7
HG_MXU_LEVELS = (5,)


def _silu(t):
    return t * jax.nn.sigmoid(t)


def _dot(a, b):
    return jnp.dot(a, b, preferred_element_type=F32)


def _dot_nt(a, b):
    return lax.dot_general(a, b, (((1,), (1,)), ((), ())), preferred_element_type=F32)


def _dot_tn(a, b):
    return lax.dot_general(a, b, (((0,), (0,)), ((), ())), preferred_element_type=F32)


def _params(n_axes):
    return pltpu.CompilerParams(
        dimension_semantics=("arbitrary",) * n_axes, vmem_limit_bytes=VMEM_LIMIT)


def _mod_kernel(c_ref, w_ref, b_ref, o_ref):
    sc = _silu(c_ref[...])
    o_ref[...] = jnp.sum(sc * w_ref[...], axis=0, keepdims=True) + b_ref[...]


def _mod_call(c_col, w_ada, b_ada, layer):
    d, n = w_ada.shape[1], w_ada.shape[2]
    tn = 1024
    return pl.pallas_call(
        _mod_kernel,
        grid=(n // tn,),
        in_specs=[
            pl.BlockSpec((d, 1), lambda j: (0, 0)),
            pl.BlockSpec((None, d, tn), lambda j: (layer, 0, j)),
            pl.BlockSpec((None, 1, tn), lambda j: (layer, 0, j)),
        ],
        out_specs=pl.BlockSpec((1, tn), lambda j: (0, j)),
        out_shape=jax.ShapeDtypeStruct((1, n), F32),
        compiler_params=_params(1),
        name="mod",
    )(c_col, w_ada, b_ada.reshape(b_ada.shape[0], 1, n))


def _rope_kernel(pos_ref, f_ref, cos_ref, sin_ref):
    ang = pos_ref[...].astype(F32) * f_ref[...]
    cos_ref[...] = jnp.cos(ang)
    sin_ref[...] = jnp.sin(ang)


def _rope_call(pos_col, inv_freq):
    s = pos_col.shape[0]
    hd = inv_freq.shape[1]
    ts = min(s, 2048)
    return pl.pallas_call(
        _rope_kernel,
        grid=(s // ts,),
        in_specs=[pl.BlockSpec((ts, 1), lambda i: (i, 0)),
                  pl.BlockSpec((1, hd), lambda i: (0, 0))],
        out_specs=[pl.BlockSpec((ts, hd), lambda i: (i, 0))] * 2,
        out_shape=[jax.ShapeDtypeStruct((s, hd), F32)] * 2,
        compiler_params=_params(1),
        name="rope_table",
    )(pos_col, inv_freq)


def _norm_mod(x, g, scale, shift):
    y = x * lax.rsqrt(jnp.mean(x * x, axis=-1, keepdims=True) + NORM_EPS)
    return (y * g) * (1.0 + scale) + shift


def _in_kernel(x_ref, g_ref, sc_ref, sh_ref, w_ref, o_ref, h_even_ref, h_odd_ref, *, n_sub):
    i = pl.program_id(0)
    j = pl.program_id(1)
    rb = x_ref.shape[0]

    def norm_sub_block(h_ref):
        row0 = pl.multiple_of(jnp.minimum(j, n_sub - 1) * rb, rb)
        h_ref[pl.ds(row0, rb), :] = _norm_mod(
            x_ref[...], g_ref[...], sc_ref[...], sh_ref[...]).astype(BF16)

    @pl.when(i == 0)
    def _():
        norm_sub_block(h_even_ref)
        o_ref[...] = jnp.zeros_like(o_ref)

    @pl.when(i % 2 == 1)
    def _():
        norm_sub_block(h_odd_ref)
        o_ref[...] = _dot(h_even_ref[...], w_ref[...]).astype(o_ref.dtype)

    @pl.when((i > 0) & (i % 2 == 0))
    def _():
        norm_sub_block(h_even_ref)
        o_ref[...] = _dot(h_odd_ref[...], w_ref[...]).astype(o_ref.dtype)


IN_TN = 2048
IN_ROW_BLOCK = 256


def _in_call(x, g, scale, shift, w, layer):
    s, d = x.shape
    tn = IN_TN
    nj = w.shape[2] // tn
    tm = min(s, 1024)
    rb = min(tm, IN_ROW_BLOCK)
    n_sub = tm // rb
    nt = s // tm
    assert n_sub <= nj
    vec = pl.BlockSpec((1, d), lambda i, j: (0, 0))

    def x_map(i, j):
        return (jnp.minimum(i, nt - 1) * n_sub + jnp.minimum(j, n_sub - 1), 0)

    return pl.pallas_call(
        functools.partial(_in_kernel, n_sub=n_sub),
        grid=(nt + 1, nj),
        in_specs=[
            pl.BlockSpec((rb, d), x_map),
            pl.BlockSpec((None, 1, d), lambda i, j: (layer, 0, 0)),
            vec, vec,
            pl.BlockSpec((None, d, tn), lambda i, j: (layer, 0, j)),
        ],
        out_specs=pl.BlockSpec((tm, tn), lambda i, j: ((i + nt) % (nt + 1), j)),
        out_shape=jax.ShapeDtypeStruct((s + tm, nj * tn), BF16),
        scratch_shapes=[pltpu.VMEM((tm, d), BF16), pltpu.VMEM((tm, d), BF16)],
        compiler_params=_params(2),
        name="in_proj",
    )(x, g.reshape(g.shape[0], 1, d), scale, shift, w)


def _ret_tables():
    gamma = 1.0 - jnp.exp2(-5.0 - jnp.arange(RET_H, dtype=F32))
    log_g = jnp.log(gamma)
    idx = jnp.arange(CHUNK, dtype=F32)
    diff = idx[:, None] - idx[None, :]
    decay = jnp.where(diff[None] >= 0,
                      jnp.exp(jnp.maximum(diff, 0.0)[None] * log_g[:, None, None]), 0.0)
    xi = jnp.exp((idx[None, :] + 1.0) * log_g[:, None])[:, :, None]
    zeta = jnp.exp((CHUNK - 1.0 - idx[None, :]) * log_g[:, None])[:, :, None]
    g_chunk = jnp.exp(CHUNK * log_g)[:, None, None]
    return (decay, jnp.broadcast_to(xi, (RET_H, CHUNK, RET_DV)),
            jnp.broadcast_to(zeta, (RET_H, CHUNK, RET_DK)),
            jnp.broadcast_to(g_chunk, (RET_H, 1, RET_DV)))


def _hg_tables():
    t = np.arange(CHUNK)
    sums = [t[None, :] <= t[:, None]]
    upper, pair = [], []
    for l in range(HG_LEVELS):
        w = (CHUNK // 2) >> l
        up = (t // w) % 2 == 1
        upper.append(np.broadcast_to(up[:, None], (CHUNK, HG_DK)))
        pair.append(up[:, None] & ~up[None, :] & ((t[:, None] // (2 * w)) == (t[None, :] // (2 * w))))
        if l in HG_MXU_LEVELS:
            m = (t // (2 * w)) * (2 * w) + w - 1
            lo = np.minimum(t, m)[:, None]
            hi = np.maximum(t, m)[:, None]
            sums.append((t[None, :] > lo) & (t[None, :] <= hi))
    pair.append(t[:, None] == t[None, :])
    tsum = np.concatenate(sums, axis=0).astype(np.float32)
    return (jnp.asarray(tsum, BF16), jnp.asarray(np.stack(upper), F32),
            jnp.asarray(np.stack(pair), F32))


def _ret_head(h, row0, p_ref, cos, sin, dec_ref, xi_ref, zeta_ref, gch_ref, r_ref, o_ref):
    half = RET_DK // 2

    def cols(part):
        c0 = (part * RET_H + h) * RET_DK
        return p_ref[row0:row0 + CHUNK, c0:c0 + RET_DK]

    def rope(t):
        t1 = t[:, :half]
        t2 = t[:, half:]
        return jnp.concatenate([t1 * cos - t2 * sin, t2 * cos + t1 * sin], axis=-1)

    q = rope(cols(0).astype(F32))
    k = rope(cols(1).astype(F32)) * (RET_DK ** -0.5)
    v = cols(2)

    qb = q.astype(BF16)
    inner = _dot_nt(qb, k.astype(BF16)) * dec_ref[h]
    r = r_ref[h]
    o = _dot(inner.astype(BF16), v) + _dot(qb, r.astype(BF16)) * xi_ref[h]
    r_ref[h] = r * gch_ref[h] + _dot_tn((k * zeta_ref[h]).astype(BF16), v)

    oc = o - jnp.mean(o, axis=-1, keepdims=True)
    on = oc * lax.rsqrt(jnp.mean(oc * oc, axis=-1, keepdims=True) + HEAD_EPS)
    o_ref[row0:row0 + CHUNK, h * RET_DV:(h + 1) * RET_DV] = (
        on * _silu(cols(3).astype(F32))).astype(o_ref.dtype)


def _hg_head(h, row0, col0, p_ref, lb, tsum_ref, up_ref, pair_ref, st_ref, o_ref):
    def cols(part):
        c0 = col0 + (part * HG_H + h) * HG_DK
        return p_ref[row0:row0 + CHUNK, c0:c0 + HG_DK]

    q = _silu(cols(0).astype(F32))
    f = lb + (1.0 - lb) * jax.nn.sigmoid(cols(1).astype(F32))
    k = 1.0 - f
    v = cols(2)

    l2 = jnp.log2(f)
    l_hi = l2.astype(BF16)
    l_lo = (l2 - l_hi.astype(F32)).astype(BF16)
    sums2 = _dot(tsum_ref[...], jnp.concatenate([l_hi, l_lo], axis=1))
    sums = sums2[:, :HG_DK] + sums2[:, HG_DK:]
    b = sums[0:CHUNK]

    def rows(t, r0, n):
        return t[r0:r0 + n]

    g = F32_SUBLANES
    diag = pair_ref[HG_LEVELS] * jnp.sum(q * k, axis=-1, keepdims=True)
    a_rows = [rows(diag, r, g) for r in range(0, CHUNK, g)]
    qk = q - k
    mxu_level = 0
    for l in range(HG_LEVELS):
        w = (CHUNK // 2) >> l
        if w >= g:
            xs, ups = [], []
            for r0 in range(0, CHUNK, 2 * w):
                b_mid = rows(b, r0 + w - 1, 1)
                xs.append(rows(k, r0, w) * jnp.exp2(b_mid - rows(b, r0, w)))
                xs.append(rows(q, r0 + w, w) * jnp.exp2(rows(b, r0 + w, w) - b_mid))
                ups.append(xs[-1])
            x = jnp.concatenate(xs, axis=0).astype(BF16)
            if w >= BF16_SUBLANES:
                z = _dot_nt(jnp.concatenate(ups, axis=0).astype(BF16), x)
                up_starts = [r0 + w for r0 in range(0, CHUNK, 2 * w)]
                for n, r0 in enumerate(up_starts):
                    for r in range(0, w, g):
                        a_rows[(r0 + r) // g] = (a_rows[(r0 + r) // g]
                                                 + pair_ref[l, r0 + r:r0 + r + g, :]
                                                 * rows(z, n * w + r, g))
                continue
        elif l in HG_MXU_LEVELS:
            mxu_level += 1
            x = ((k + up_ref[l] * qk) * jnp.exp2(rows(sums, mxu_level * CHUNK, CHUNK))).astype(BF16)
        elif 2 * w == g:
            b_mid = jnp.concatenate(
                [jnp.broadcast_to(rows(b, r0 + w - 1, 1), (2 * w, HG_DK))
                 for r0 in range(0, CHUNK, 2 * w)], axis=0)
            sign = 2.0 * up_ref[l] - 1.0
            x = ((k + up_ref[l] * qk) * jnp.exp2((b - b_mid) * sign)).astype(BF16)
        else:
            assert w == 1
            x = (k + up_ref[l] * (q * f - k)).astype(BF16)
        z = _dot_nt(x, x)
        for r in range(0, CHUNK, g):
            a_rows[r // g] = a_rows[r // g] + pair_ref[l, r:r + g, :] * rows(z, r, g)
    a = jnp.concatenate(a_rows, axis=0)

    e_b = jnp.exp2(b)
    e_last = jnp.exp2(rows(b, CHUNK - 1, 1) - b)
    st = st_ref[h]
    o = _dot(a.astype(BF16), v) + _dot_nt((q * e_b).astype(BF16), st.astype(BF16))
    st_ref[h] = st * rows(e_b, CHUNK - 1, 1) + _dot_tn(v, (k * e_last).astype(BF16))

    on = o * lax.rsqrt(jnp.mean(o * o, axis=-1, keepdims=True) + HEAD_EPS)
    o_ref[row0:row0 + CHUNK, h * HG_DV:(h + 1) * HG_DV] = (
        on * _silu(cols(3).astype(F32))).astype(o_ref.dtype)


MIX_ROWS = 2 * CHUNK


def _mixers(layer, hg_col0, p_ref, cos_ref, sin_ref, dec_ref, xi_ref, zeta_ref, gch_ref,
            lbp_ref, tsum_ref, up_ref, pair_ref, r_ref, st_ref, oret_ref, ohg_ref):
    lbp = lbp_ref[...]
    e = jnp.exp(lbp - jnp.max(lbp, axis=0, keepdims=True))
    lb_all = jnp.sum(e[:layer + 1], axis=0, keepdims=True) / jnp.sum(e, axis=0, keepdims=True)
    for r0 in range(0, p_ref.shape[0], CHUNK):
        cos = cos_ref[r0:r0 + CHUNK, :]
        sin = sin_ref[r0:r0 + CHUNK, :]
        for h in range(RET_H):
            _ret_head(h, r0, p_ref, cos, sin, dec_ref, xi_ref, zeta_ref, gch_ref, r_ref,
                      oret_ref)
        for h in range(HG_H):
            _hg_head(h, r0, hg_col0, p_ref, lb_all[:, h * HG_DK:(h + 1) * HG_DK],
                     tsum_ref, up_ref, pair_ref, st_ref, ohg_ref)


def _merge(oret_ref, ohg_ref, gab_ref, x_ref, gate_ref, g2_ref, sc2_ref, sh2_ref,
           wa_ref, wb_ref, wo_ref, x1_ref, h2_ref):
    d = x_ref.shape[1]
    ya = _dot(oret_ref[...], wa_ref[...])
    yb = _dot(ohg_ref[...], wb_ref[...])
    merged = (jax.nn.sigmoid(gab_ref[:, :d].astype(F32)) * ya
              + jax.nn.sigmoid(gab_ref[:, d:].astype(F32)) * yb)
    x1 = x_ref[...] + gate_ref[...] * _dot(merged.astype(BF16), wo_ref[...])
    x1_ref[...] = x1
    h2_ref[...] = _norm_mod(x1, g2_ref[...], sc2_ref[...], sh2_ref[...]).astype(BF16)


def _mixmerge_kernel(layer, hg_col0, n_steps, p_ref, cos_ref, sin_ref, dec_ref, xi_ref,
                     zeta_ref, gch_ref, lbp_ref, tsum_ref, up_ref, pair_ref,
                     gab_ref, x_ref, gate_ref, g2_ref, sc2_ref, sh2_ref, wa_ref, wb_ref, wo_ref,
                     x1_ref, h2_ref, r_ref, st_ref, oa_even, ob_even, oa_odd, ob_odd):
    c = pl.program_id(0)

    @pl.when(c == 0)
    def _():
        r_ref[...] = jnp.zeros_like(r_ref)
        st_ref[...] = jnp.zeros_like(st_ref)

    mix_in = (p_ref, cos_ref, sin_ref, dec_ref, xi_ref, zeta_ref, gch_ref, lbp_ref, tsum_ref,
              up_ref, pair_ref, r_ref, st_ref)
    merge_in = (gab_ref, x_ref, gate_ref, g2_ref, sc2_ref, sh2_ref, wa_ref, wb_ref, wo_ref,
                x1_ref, h2_ref)
    bufs = ((oa_even, ob_even), (oa_odd, ob_odd))

    @pl.when(c == 0)
    def _():
        _mixers(layer, hg_col0, *mix_in, *bufs[0])

    for parity in (0, 1):
        @pl.when((c > 0) & (c < n_steps) & (c % 2 == parity))
        def _():
            _mixers(layer, hg_col0, *mix_in, *bufs[parity])
            _merge(*bufs[1 - parity], *merge_in)

    @pl.when(c == n_steps)
    def _():
        _merge(*bufs[(n_steps - 1) % 2], *merge_in)


def _mixmerge_call(p, cos, sin, hg_lb, x, gate, g2, scale2, shift2, w_ret_o, w_hg_o, w_out,
                   layer):
    s, d = x.shape
    tm = min(s, MIX_ROWS)
    n_steps = s // tm
    ret_cols = 4 * RET_H * RET_DK
    mix_cols = ret_cols + 4 * HG_H * HG_DK
    assert mix_cols % (2 * d) == 0
    gate_block = mix_cols // (2 * d)
    tables = _ret_tables() + (hg_lb,) + _hg_tables()

    def const(shape):
        return pl.BlockSpec(shape, lambda c: (0,) * len(shape), pipeline_mode=pl.Buffered(1))

    def layer_const(w):
        return pl.BlockSpec((None,) + w.shape[1:], lambda c: (layer,) + (0,) * (w.ndim - 1),
                            pipeline_mode=pl.Buffered(1))

    def cur(c):
        return jnp.minimum(c, n_steps - 1)

    def prev(c):
        return jnp.maximum(c - 1, 0)

    ret_w, hg_w = RET_H * RET_DV, HG_H * HG_DV
    return pl.pallas_call(
        functools.partial(_mixmerge_kernel, layer, ret_cols, n_steps),
        grid=(n_steps + 1,),
        in_specs=[pl.BlockSpec((tm, mix_cols), lambda c: (cur(c), 0)),
                  pl.BlockSpec((tm, RET_DK // 2), lambda c: (cur(c), 0)),
                  pl.BlockSpec((tm, RET_DK // 2), lambda c: (cur(c), 0))]
        + [const(t.shape) for t in tables]
        + [pl.BlockSpec((tm, 2 * d), lambda c: (prev(c), gate_block)),
           pl.BlockSpec((tm, d), lambda c: (prev(c), 0)),
           const((1, d)), layer_const(g2.reshape(g2.shape[0], 1, d)), const((1, d)),
           const((1, d)), layer_const(w_ret_o), layer_const(w_hg_o), layer_const(w_out)],
        out_specs=[pl.BlockSpec((tm, d), lambda c: (prev(c), 0))] * 2,
        out_shape=[jax.ShapeDtypeStruct((s, d), F32), jax.ShapeDtypeStruct((s, d), BF16)],
        scratch_shapes=[pltpu.VMEM((RET_H, RET_DK, RET_DV), F32),
                        pltpu.VMEM((HG_H, HG_DV, HG_DK), F32),
                        pltpu.VMEM((tm, ret_w), BF16), pltpu.VMEM((tm, hg_w), BF16),
                        pltpu.VMEM((tm, ret_w), BF16), pltpu.VMEM((tm, hg_w), BF16)],
        compiler_params=_params(1),
        name="mixers_merge",
    )(p, cos, sin, *tables, p, x, gate, g2.reshape(g2.shape[0], 1, d), scale2, shift2,
      w_ret_o, w_hg_o, w_out)


FFN_OUT_ROWS = 512


def _ffn_kernel(h_hbm, x_ref, gate_ref, gf_ref, wa_ref, wb_ref, cwa_ref, cwb_ref, cba_ref,
                cbb_ref, wd_ref, o_ref, hx_ref, acc_ref, hsem, *, nj, final_norm):
    i = pl.program_id(0)
    j = pl.program_id(1)
    nt = pl.num_programs(0)
    halo = BF16_SUBLANES
    tm = acc_ref.shape[0]
    slot = i % 2

    def first_tile_copy():
        return pltpu.make_async_copy(h_hbm.at[pl.ds(0, tm), :],
                                     hx_ref.at[0, pl.ds(halo, tm), :], hsem.at[0])

    def tile_copy(t, s):
        row0 = pl.multiple_of(t * tm - halo, halo)
        return pltpu.make_async_copy(h_hbm.at[pl.ds(row0, tm + halo), :], hx_ref.at[s],
                                     hsem.at[s])

    @pl.when(j == 0)
    def _():
        @pl.when(i == 0)
        def _():
            hx_ref[0, 0:halo, :] = jnp.zeros((halo, hx_ref.shape[2]), BF16)
            first_tile_copy().start()
            first_tile_copy().wait()

        @pl.when(i > 0)
        def _():
            tile_copy(i, slot).wait()

        @pl.when(i + 1 < nt)
        def _():
            tile_copy(i + 1, 1 - slot).start()

        acc_ref[...] = jnp.zeros_like(acc_ref)

    @pl.when(j < nj)
    def _():
        hx = hx_ref[slot]

        def conv(u, cw_ref, cb_ref):
            cw = cw_ref[...]
            y = (cw[0:1] * pltpu.roll(u, 2, 0) + cw[1:2] * pltpu.roll(u, 1, 0)
                 + cw[2:3] * u + cb_ref[...])
            return y[halo:, :]

        a = conv(_dot(hx, wa_ref[...]), cwa_ref, cba_ref)
        b = conv(_dot(hx, wb_ref[...]), cwb_ref, cbb_ref)
        acc_ref[...] += _dot((_silu(a) * b).astype(BF16), wd_ref[...])

    @pl.when(j >= nj)
    def _():
        rows = o_ref.shape[0]
        row0 = pl.multiple_of((j - nj) * rows, rows)
        x2 = x_ref[...] + gate_ref[...] * acc_ref[pl.ds(row0, rows), :]
        if final_norm:
            x2 = (x2 * lax.rsqrt(jnp.mean(x2 * x2, axis=-1, keepdims=True) + NORM_EPS)
                  * gf_ref[...])
        o_ref[...] = x2


def _ffn_call(x, h, gate, g_final, w_up, conv_w, conv_b, w_down, layer, final_norm):
    s, d = x.shape
    dff = w_down.shape[1]
    tm = min(s, 1024)
    tf = 512
    nj = dff // tf
    ro = min(tm, FFN_OUT_ROWS)
    n_out = tm // ro
    halo = BF16_SUBLANES
    assert CONV_W - 1 <= halo
    vec = pl.BlockSpec((1, d), lambda i, j: (0, 0))

    def cols(shape, off):
        return pl.BlockSpec((None,) + shape,
                            lambda i, j: (layer, 0, off + jnp.minimum(j, nj - 1)))

    def out_rows(i, j):
        return (jnp.maximum(i * n_out + jnp.maximum(j - nj, -1), 0), 0)

    conv_b3 = conv_b.reshape(conv_b.shape[0], 1, 2 * dff)
    return pl.pallas_call(
        functools.partial(_ffn_kernel, nj=nj, final_norm=final_norm),
        grid=(s // tm, nj + n_out),
        in_specs=[
            pl.BlockSpec(memory_space=pl.ANY),
            pl.BlockSpec((ro, d), out_rows),
            vec, vec,
            cols((d, tf), 0), cols((d, tf), nj),
            cols((CONV_W, tf), 0), cols((CONV_W, tf), nj),
            cols((1, tf), 0), cols((1, tf), nj),
            pl.BlockSpec((None, tf, d), lambda i, j: (layer, jnp.minimum(j, nj - 1), 0)),
        ],
        out_specs=pl.BlockSpec((ro, d), out_rows),
        out_shape=jax.ShapeDtypeStruct((s, d), F32),
        scratch_shapes=[pltpu.VMEM((2, tm + halo, d), BF16), pltpu.VMEM((tm, d), F32),
                        pltpu.SemaphoreType.DMA((2,))],
        compiler_params=_params(2),
        name="conv_ffn",
    )(h, x, gate, g_final.reshape(1, d), w_up, w_up, conv_w, conv_w, conv_b3, conv_b3, w_down)


def kernel(x, c, positions, w_ada, b_ada, g_norm1, w_in, w_ret_o, w_hg_o, w_out, hg_lb,
           g_norm2, w_up, conv_w, conv_b, w_down, g_final):
    batch, s, d = x.shape
    depth = w_in.shape[0]
    assert batch == 1 and s % CHUNK == 0

    xs = x.reshape(s, d)
    inv_freq = ROPE_BASE ** (-jnp.arange(0, RET_DK, 2, dtype=F32) / RET_DK)
    cos, sin = _rope_call(positions.reshape(s, 1), inv_freq.reshape(1, RET_DK // 2))

    w_in_b = w_in.astype(BF16)
    w_ret_o_b = w_ret_o.astype(BF16)
    w_hg_o_b = w_hg_o.astype(BF16)
    w_out_b = w_out.astype(BF16)
    w_up_b = w_up.astype(BF16)
    w_down_b = w_down.astype(BF16)

    for l in range(depth):
        mod = _mod_call(c.reshape(d, 1), w_ada, b_ada, l)
        shift1, scale1, gate1, shift2, scale2, gate2 = [
            mod[:, n * d:(n + 1) * d] for n in range(6)]
        p = _in_call(xs, g_norm1, scale1, shift1, w_in_b, l)
        xs, h2 = _mixmerge_call(p, cos, sin, hg_lb, xs, gate1, g_norm2, scale2, shift2,
                                w_ret_o_b, w_hg_o_b, w_out_b, l)
        xs = _ffn_call(xs, h2, gate2, g_final, w_up_b, conv_w, conv_b, w_down_b, l,
                       final_norm=(l == depth - 1))
    return xs.reshape(batch, s, d)
```

```python
import functools

import numpy as np
import jax
import jax.numpy as jnp
from jax import lax
from jax.experimental import pallas as pl
from jax.experimental.pallas import tpu as pltpu

F32 = jnp.float32
BF16 = jnp.bfloat16

RET_H = 4
RET_DK = 256
RET_DV = 256
HG_H = 8
HG_DK = 128
HG_DV = 128
CHUNK = 128
CONV_W = 3
ROPE_BASE = 10000.0
NORM_EPS = 1e-6
HEAD_EPS = 1e-5

V7X_VMEM_BYTES = 64 * 1024 * 1024
VMEM_LIMIT = V7X_VMEM_BYTES - 6 * 1024 * 1024
BF16_SUBLANES = 16

F32_SUBLANES = 8
HG_LEVELS = 7
HG_MXU_LEVELS = (5,)


def _silu(t):
    return t * jax.nn.sigmoid(t)


def _dot(a, b):
    return jnp.dot(a, b, preferred_element_type=F32)


def _dot_nt(a, b):
    return lax.dot_general(a, b, (((1,), (1,)), ((), ())), preferred_element_type=F32)


def _dot_tn(a, b):
    return lax.dot_general(a, b, (((0,), (0,)), ((), ())), preferred_element_type=F32)


def _params(n_axes):
    return pltpu.CompilerParams(
        dimension_semantics=("arbitrary",) * n_axes, vmem_limit_bytes=VMEM_LIMIT)


def _mod_kernel(c_ref, w_ref, b_ref, o_ref):
    sc = _silu(c_ref[...])
    o_ref[...] = jnp.sum(sc * w_ref[...], axis=0, keepdims=True) + b_ref[...]


def _mod_call(c_col, w_ada, b_ada, layer):
    d, n = w_ada.shape[1], w_ada.shape[2]
    tn = 1024
    return pl.pallas_call(
        _mod_kernel,
        grid=(n // tn,),
        in_specs=[
            pl.BlockSpec((d, 1), lambda j: (0, 0)),
            pl.BlockSpec((None, d, tn), lambda j: (layer, 0, j)),
            pl.BlockSpec((None, 1, tn), lambda j: (layer, 0, j)),
        ],
        out_specs=pl.BlockSpec((1, tn), lambda j: (0, j)),
        out_shape=jax.ShapeDtypeStruct((1, n), F32),
        compiler_params=_params(1),
        name="mod",
    )(c_col, w_ada, b_ada.reshape(b_ada.shape[0], 1, n))


def _rope_kernel(pos_ref, f_ref, cos_ref, sin_ref):
    ang = pos_ref[...].astype(F32) * f_ref[...]
    cos_ref[...] = jnp.cos(ang)
    sin_ref[...] = jnp.sin(ang)


def _rope_call(pos_col, inv_freq):
    s = pos_col.shape[0]
    hd = inv_freq.shape[1]
    ts = min(s, 2048)
    return pl.pallas_call(
        _rope_kernel,
        grid=(s // ts,),
        in_specs=[pl.BlockSpec((ts, 1), lambda i: (i, 0)),
                  pl.BlockSpec((1, hd), lambda i: (0, 0))],
        out_specs=[pl.BlockSpec((ts, hd), lambda i: (i, 0))] * 2,
        out_shape=[jax.ShapeDtypeStruct((s, hd), F32)] * 2,
        compiler_params=_params(1),
        name="rope_table",
    )(pos_col, inv_freq)


def _norm_mod(x, g, scale, shift):
    y = x * lax.rsqrt(jnp.mean(x * x, axis=-1, keepdims=True) + NORM_EPS)
    return (y * g) * (1.0 + scale) + shift


def _in_kernel(x_ref, g_ref, sc_ref, sh_ref, w_ref, ca_hbm, cb_hbm, o_ref, ca_out, cb_out,
               h_even_ref, h_odd_ref, cin_ref, cout_ref, insem, outsem, *, n_sub, n_a, n_b):
    i = pl.program_id(0)
    j = pl.program_id(1)
    nj = pl.num_programs(1)
    rb = x_ref.shape[0]
    rc = cin_ref.shape[1]
    n_chunks = n_a + n_b

    def norm_sub_block(h_ref):
        row0 = pl.multiple_of(jnp.minimum(j, n_sub - 1) * rb, rb)
        h_ref[pl.ds(row0, rb), :] = _norm_mod(
            x_ref[...], g_ref[...], sc_ref[...], sh_ref[...]).astype(BF16)

    def chunk_dma(k, write_back, op):
        slot = k % 2
        for first, count, src, dst in ((0, n_a, ca_hbm, ca_out), (n_a, n_b, cb_hbm, cb_out)):
            @pl.when((k >= first) & (k < first + count))
            def _():
                rows = pl.ds(pl.multiple_of((k - first) * rc, BF16_SUBLANES), rc)
                if write_back:
                    copy = pltpu.make_async_copy(cout_ref.at[slot], dst.at[rows, :],
                                                 outsem.at[slot])
                else:
                    copy = pltpu.make_async_copy(src.at[rows, :], cin_ref.at[slot],
                                                 insem.at[slot])
                getattr(copy, op)()

    k = (i - 1) * nj + j

    @pl.when(i == 0)
    def _():
        norm_sub_block(h_even_ref)
        o_ref[...] = jnp.zeros_like(o_ref)

        @pl.when(j == nj - 1)
        def _():
            chunk_dma(k + 1, False, "start")

    @pl.when(i > 0)
    def _():
        chunk_dma(k, False, "wait")

        @pl.when(k + 1 < n_chunks)
        def _():
            chunk_dma(k + 1, False, "start")

        @pl.when(k >= 2)
        def _():
            chunk_dma(k - 2, True, "wait")

    def cast_chunk():
        cout_ref[k % 2] = cin_ref[k % 2].astype(BF16)

    @pl.when(i % 2 == 1)
    def _():
        cast_chunk()
        norm_sub_block(h_odd_ref)
        o_ref[...] = _dot(h_even_ref[...], w_ref[...]).astype(o_ref.dtype)

    @pl.when((i > 0) & (i % 2 == 0))
    def _():
        cast_chunk()
        norm_sub_block(h_even_ref)
        o_ref[...] = _dot(h_odd_ref[...], w_ref[...]).astype(o_ref.dtype)

    @pl.when(i > 0)
    def _():
        chunk_dma(k, True, "start")

        @pl.when(k == n_chunks - 1)
        def _():
            chunk_dma(k - 1, True, "wait")
            chunk_dma(k, True, "wait")


IN_TN = 2048
IN_ROW_BLOCK = 256


def _in_call(x, g, scale, shift, w, layer, cast_a, cast_b):
    s, d = x.shape
    tn = IN_TN
    nj = w.shape[2] // tn
    tm = min(s, 1024)
    rb = min(tm, IN_ROW_BLOCK)
    n_sub = tm // rb
    nt = s // tm
    assert n_sub <= nj
    n_chunks = nt * nj
    rc = (cast_a.shape[0] + cast_b.shape[0]) // n_chunks
    n_a, n_b = cast_a.shape[0] // rc, cast_b.shape[0] // rc
    assert n_a * rc == cast_a.shape[0] and n_b * rc == cast_b.shape[0]
    assert n_a + n_b == n_chunks >= 2 and rc % BF16_SUBLANES == 0
    vec = pl.BlockSpec((1, d), lambda i, j: (0, 0))
    hbm = pl.BlockSpec(memory_space=pl.ANY)

    def x_map(i, j):
        return (jnp.minimum(i, nt - 1) * n_sub + jnp.minimum(j, n_sub - 1), 0)

    return pl.pallas_call(
        functools.partial(_in_kernel, n_sub=n_sub, n_a=n_a, n_b=n_b),
        grid=(nt + 1, nj),
        in_specs=[
            pl.BlockSpec((rb, d), x_map),
            pl.BlockSpec((None, 1, d), lambda i, j: (layer, 0, 0)),
            vec, vec,
            pl.BlockSpec((None, d, tn), lambda i, j: (layer, 0, j)),
            hbm, hbm,
        ],
        out_specs=[pl.BlockSpec((tm, tn), lambda i, j: ((i + nt) % (nt + 1), j)), hbm, hbm],
        out_shape=[jax.ShapeDtypeStruct((s + tm, nj * tn), BF16),
                   jax.ShapeDtypeStruct(cast_a.shape, BF16),
                   jax.ShapeDtypeStruct(cast_b.shape, BF16)],
        scratch_shapes=[pltpu.VMEM((tm, d), BF16), pltpu.VMEM((tm, d), BF16),
                        pltpu.VMEM((2, rc, cast_a.shape[1]), F32),
                        pltpu.VMEM((2, rc, cast_a.shape[1]), BF16),
                        pltpu.SemaphoreType.DMA((2,)), pltpu.SemaphoreType.DMA((2,))],
        compiler_params=_params(2),
        name="in_proj",
    )(x, g.reshape(g.shape[0], 1, d), scale, shift, w, cast_a, cast_b)


def _ret_tables():
    gamma = 1.0 - jnp.exp2(-5.0 - jnp.arange(RET_H, dtype=F32))
    log_g = jnp.log(gamma)
    idx = jnp.arange(CHUNK, dtype=F32)
    diff = idx[:, None] - idx[None, :]
    decay = jnp.where(diff[None] >= 0,
                      jnp.exp(jnp.maximum(diff, 0.0)[None] * log_g[:, None, None]), 0.0)
    xi = jnp.exp((idx[None, :] + 1.0) * log_g[:, None])[:, :, None]
    zeta = jnp.exp((CHUNK - 1.0 - idx[None, :]) * log_g[:, None])[:, :, None]
    g_chunk = jnp.exp(CHUNK * log_g)[:, None, None]
    return (decay, jnp.broadcast_to(xi, (RET_H, CHUNK, RET_DV)),
            jnp.broadcast_to(zeta, (RET_H, CHUNK, RET_DK)),
            jnp.broadcast_to(g_chunk, (RET_H, 1, RET_DV)))


def _hg_tables():
    t = np.arange(CHUNK)
    sums = [t[None, :] <= t[:, None]]
    upper, pair = [], []
    for l in range(HG_LEVELS):
        w = (CHUNK // 2) >> l
        up = (t // w) % 2 == 1
        upper.append(np.broadcast_to(up[:, None], (CHUNK, HG_DK)))
        pair.append(up[:, None] & ~up[None, :] & ((t[:, None] // (2 * w)) == (t[None, :] // (2 * w))))
        if l in HG_MXU_LEVELS:
            m = (t // (2 * w)) * (2 * w) + w - 1
            lo = np.minimum(t, m)[:, None]
            hi = np.maximum(t, m)[:, None]
            sums.append((t[None, :] > lo) & (t[None, :] <= hi))
    pair.append(t[:, None] == t[None, :])
    tsum = np.concatenate(sums, axis=0).astype(np.float32)
    return (jnp.asarray(tsum, BF16), jnp.asarray(np.stack(upper), F32),
            jnp.asarray(np.stack(pair), F32))


def _ret_head(h, row0, p_ref, cos, sin, dec_ref, xi_ref, zeta_ref, gch_ref, r_ref, o_ref):
    half = RET_DK // 2

    def cols(part):
        c0 = (part * RET_H + h) * RET_DK
        return p_ref[row0:row0 + CHUNK, c0:c0 + RET_DK]

    def rope(t):
        t1 = t[:, :half]
        t2 = t[:, half:]
        return jnp.concatenate([t1 * cos - t2 * sin, t2 * cos + t1 * sin], axis=-1)

    q = rope(cols(0).astype(F32))
    k = rope(cols(1).astype(F32)) * (RET_DK ** -0.5)
    v = cols(2)

    qb = q.astype(BF16)
    inner = _dot_nt(qb, k.astype(BF16)) * dec_ref[h]
    r = r_ref[h]
    o = _dot(inner.astype(BF16), v) + _dot(qb, r.astype(BF16)) * xi_ref[h]
    r_ref[h] = r * gch_ref[h] + _dot_tn((k * zeta_ref[h]).astype(BF16), v)

    oc = o - jnp.mean(o, axis=-1, keepdims=True)
    on = oc * lax.rsqrt(jnp.mean(oc * oc, axis=-1, keepdims=True) + HEAD_EPS)
    o_ref[row0:row0 + CHUNK, h * RET_DV:(h + 1) * RET_DV] = (
        on * _silu(cols(3).astype(F32))).astype(o_ref.dtype)


def _hg_head(h, row0, col0, p_ref, lb, tsum_ref, up_ref, pair_ref, st_ref, o_ref):
    def cols(part):
        c0 = col0 + (part * HG_H + h) * HG_DK
        return p_ref[row0:row0 + CHUNK, c0:c0 + HG_DK]

    q = _silu(cols(0).astype(F32))
    f = lb + (1.0 - lb) * jax.nn.sigmoid(cols(1).astype(F32))
    k = 1.0 - f
    v = cols(2)

    l2 = jnp.log2(f)
    l_hi = l2.astype(BF16)
    l_lo = (l2 - l_hi.astype(F32)).astype(BF16)
    sums2 = _dot(tsum_ref[...], jnp.concatenate([l_hi, l_lo], axis=1))
    sums = sums2[:, :HG_DK] + sums2[:, HG_DK:]
    b = sums[0:CHUNK]

    def rows(t, r0, n):
        return t[r0:r0 + n]

    g = F32_SUBLANES
    diag = pair_ref[HG_LEVELS] * jnp.sum(q * k, axis=-1, keepdims=True)
    a_rows = [rows(diag, r, g) for r in range(0, CHUNK, g)]
    qk = q - k
    mxu_level = 0
    for l in range(HG_LEVELS):
        w = (CHUNK // 2) >> l
        if w >= g:
            xs, ups = [], []
            for r0 in range(0, CHUNK, 2 * w):
                b_mid = rows(b, r0 + w - 1, 1)
                xs.append(rows(k, r0, w) * jnp.exp2(b_mid - rows(b, r0, w)))
                xs.append(rows(q, r0 + w, w) * jnp.exp2(rows(b, r0 + w, w) - b_mid))
                ups.append(xs[-1])
            x = jnp.concatenate(xs, axis=0).astype(BF16)
            if w >= BF16_SUBLANES:
                z = _dot_nt(jnp.concatenate(ups, axis=0).astype(BF16), x)
                up_starts = [r0 + w for r0 in range(0, CHUNK, 2 * w)]
                for n, r0 in enumerate(up_starts):
                    for r in range(0, w, g):
                        a_rows[(r0 + r) // g] = (a_rows[(r0 + r) // g]
                                                 + pair_ref[l, r0 + r:r0 + r + g, :]
                                                 * rows(z, n * w + r, g))
                continue
        elif l in HG_MXU_LEVELS:
            mxu_level += 1
            x = ((k + up_ref[l] * qk) * jnp.exp2(rows(sums, mxu_level * CHUNK, CHUNK))).astype(BF16)
        elif 2 * w == g:
            b_mid = jnp.concatenate(
                [jnp.broadcast_to(rows(b, r0 + w - 1, 1), (2 * w, HG_DK))
                 for r0 in range(0, CHUNK, 2 * w)], axis=0)
            sign = 2.0 * up_ref[l] - 1.0
            x = ((k + up_ref[l] * qk) * jnp.exp2((b - b_mid) * sign)).astype(BF16)
        else:
            assert w == 1
            x = (k + up_ref[l] * (q * f - k)).astype(BF16)
        z = _dot_nt(x, x)
        for r in range(0, CHUNK, g):
            a_rows[r // g] = a_rows[r // g] + pair_ref[l, r:r + g, :] * rows(z, r, g)
    a = jnp.concatenate(a_rows, axis=0)

    e_b = jnp.exp2(b)
    e_last = jnp.exp2(rows(b, CHUNK - 1, 1) - b)
    st = st_ref[h]
    o = _dot(a.astype(BF16), v) + _dot_nt((q * e_b).astype(BF16), st.astype(BF16))
    st_ref[h] = st * rows(e_b, CHUNK - 1, 1) + _dot_tn(v, (k * e_last).astype(BF16))

    on = o * lax.rsqrt(jnp.mean(o * o, axis=-1, keepdims=True) + HEAD_EPS)
    o_ref[row0:row0 + CHUNK, h * HG_DV:(h + 1) * HG_DV] = (
        on * _silu(cols(3).astype(F32))).astype(o_ref.dtype)


MIX_ROWS = 2 * CHUNK


def _mixers(layer, hg_col0, p_ref, cos_ref, sin_ref, dec_ref, xi_ref, zeta_ref, gch_ref,
            lbp_ref, tsum_ref, up_ref, pair_ref, r_ref, st_ref, oret_ref, ohg_ref):
    lbp = lbp_ref[...]
    e = jnp.exp(lbp - jnp.max(lbp, axis=0, keepdims=True))
    lb_all = jnp.sum(e[:layer + 1], axis=0, keepdims=True) / jnp.sum(e, axis=0, keepdims=True)
    for r0 in range(0, p_ref.shape[0], CHUNK):
        cos = cos_ref[r0:r0 + CHUNK, :]
        sin = sin_ref[r0:r0 + CHUNK, :]
        for h in range(RET_H):
            _ret_head(h, r0, p_ref, cos, sin, dec_ref, xi_ref, zeta_ref, gch_ref, r_ref,
                      oret_ref)
        for h in range(HG_H):
            _hg_head(h, r0, hg_col0, p_ref, lb_all[:, h * HG_DK:(h + 1) * HG_DK],
                     tsum_ref, up_ref, pair_ref, st_ref, ohg_ref)


def _merge(oret_ref, ohg_ref, gab_ref, x_ref, gate_ref, g2_ref, sc2_ref, sh2_ref,
           wa_ref, wb_ref, wo_ref, x1_ref, h2_ref):
    d = x_ref.shape[1]
    ya = _dot(oret_ref[...], wa_ref[...])
    yb = _dot(ohg_ref[...], wb_ref[...])
    merged = (jax.nn.sigmoid(gab_ref[:, :d].astype(F32)) * ya
              + jax.nn.sigmoid(gab_ref[:, d:].astype(F32)) * yb)
    x1 = x_ref[...] + gate_ref[...] * _dot(merged.astype(BF16), wo_ref[...])
    x1_ref[...] = x1
    h2_ref[...] = _norm_mod(x1, g2_ref[...], sc2_ref[...], sh2_ref[...]).astype(BF16)


def _mixmerge_kernel(layer, hg_col0, n_steps, p_ref, cos_ref, sin_ref, dec_ref, xi_ref,
                     zeta_ref, gch_ref, lbp_ref, tsum_ref, up_ref, pair_ref,
                     gab_ref, x_ref, gate_ref, g2_ref, sc2_ref, sh2_ref, wa_ref, wb_ref, wo_ref,
                     x1_ref, h2_ref, r_ref, st_ref, oa_even, ob_even, oa_odd, ob_odd):
    c = pl.program_id(0)

    @pl.when(c == 0)
    def _():
        r_ref[...] = jnp.zeros_like(r_ref)
        st_ref[...] = jnp.zeros_like(st_ref)

    mix_in = (p_ref, cos_ref, sin_ref, dec_ref, xi_ref, zeta_ref, gch_ref, lbp_ref, tsum_ref,
              up_ref, pair_ref, r_ref, st_ref)
    merge_in = (gab_ref, x_ref, gate_ref, g2_ref, sc2_ref, sh2_ref, wa_ref, wb_ref, wo_ref,
                x1_ref, h2_ref)
    bufs = ((oa_even, ob_even), (oa_odd, ob_odd))

    @pl.when(c == 0)
    def _():
        _mixers(layer, hg_col0, *mix_in, *bufs[0])

    for parity in (0, 1):
        @pl.when((c > 0) & (c < n_steps) & (c % 2 == parity))
        def _():
            _mixers(layer, hg_col0, *mix_in, *bufs[parity])
            _merge(*bufs[1 - parity], *merge_in)

    @pl.when(c == n_steps)
    def _():
        _merge(*bufs[(n_steps - 1) % 2], *merge_in)


def _mixmerge_call(p, cos, sin, hg_lb, x, gate, g2, scale2, shift2, w_ret_o, w_hg_o, w_out,
                   layer):
    s, d = x.shape
    tm = min(s, MIX_ROWS)
    n_steps = s // tm
    ret_cols = 4 * RET_H * RET_DK
    mix_cols = ret_cols + 4 * HG_H * HG_DK
    assert mix_cols % (2 * d) == 0
    gate_block = mix_cols // (2 * d)
    tables = _ret_tables() + (hg_lb,) + _hg_tables()

    def const(shape):
        return pl.BlockSpec(shape, lambda c: (0,) * len(shape), pipeline_mode=pl.Buffered(1))

    def layer_const(w):
        return pl.BlockSpec((None,) + w.shape[1:], lambda c: (layer,) + (0,) * (w.ndim - 1),
                            pipeline_mode=pl.Buffered(1))

    def cur(c):
        return jnp.minimum(c, n_steps - 1)

    def prev(c):
        return jnp.maximum(c - 1, 0)

    ret_w, hg_w = RET_H * RET_DV, HG_H * HG_DV
    return pl.pallas_call(
        functools.partial(_mixmerge_kernel, layer, ret_cols, n_steps),
        grid=(n_steps + 1,),
        in_specs=[pl.BlockSpec((tm, mix_cols), lambda c: (cur(c), 0)),
                  pl.BlockSpec((tm, RET_DK // 2), lambda c: (cur(c), 0)),
                  pl.BlockSpec((tm, RET_DK // 2), lambda c: (cur(c), 0))]
        + [const(t.shape) for t in tables]
        + [pl.BlockSpec((tm, 2 * d), lambda c: (prev(c), gate_block)),
           pl.BlockSpec((tm, d), lambda c: (prev(c), 0)),
           const((1, d)), layer_const(g2.reshape(g2.shape[0], 1, d)), const((1, d)),
           const((1, d)), layer_const(w_ret_o), layer_const(w_hg_o), layer_const(w_out)],
        out_specs=[pl.BlockSpec((tm, d), lambda c: (prev(c), 0))] * 2,
        out_shape=[jax.ShapeDtypeStruct((s, d), F32), jax.ShapeDtypeStruct((s, d), BF16)],
        scratch_shapes=[pltpu.VMEM((RET_H, RET_DK, RET_DV), F32),
                        pltpu.VMEM((HG_H, HG_DV, HG_DK), F32),
                        pltpu.VMEM((tm, ret_w), BF16), pltpu.VMEM((tm, hg_w), BF16),
                        pltpu.VMEM((tm, ret_w), BF16), pltpu.VMEM((tm, hg_w), BF16)],
        compiler_params=_params(1),
        name="mixers_merge",
    )(p, cos, sin, *tables, p, x, gate, g2.reshape(g2.shape[0], 1, d), scale2, shift2,
      w_ret_o, w_hg_o, w_out)


FFN_OUT_ROWS = 512


def _ffn_kernel(h_hbm, x_ref, gate_ref, gf_ref, wa_ref, wb_ref, cwa_ref, cwb_ref, cba_ref,
                cbb_ref, wd_ref, o_ref, hx_ref, acc_ref, hsem, *, nj, final_norm):
    i = pl.program_id(0)
    j = pl.program_id(1)
    nt = pl.num_programs(0)
    halo = BF16_SUBLANES
    tm = acc_ref.shape[0]
    slot = i % 2

    def first_tile_copy():
        return pltpu.make_async_copy(h_hbm.at[pl.ds(0, tm), :],
                                     hx_ref.at[0, pl.ds(halo, tm), :], hsem.at[0])

    def tile_copy(t, s):
        row0 = pl.multiple_of(t * tm - halo, halo)
        return pltpu.make_async_copy(h_hbm.at[pl.ds(row0, tm + halo), :], hx_ref.at[s],
                                     hsem.at[s])

    @pl.when(j == 0)
    def _():
        @pl.when(i == 0)
        def _():
            hx_ref[0, 0:halo, :] = jnp.zeros((halo, hx_ref.shape[2]), BF16)
            first_tile_copy().start()
            first_tile_copy().wait()

        @pl.when(i > 0)
        def _():
            tile_copy(i, slot).wait()

        @pl.when(i + 1 < nt)
        def _():
            tile_copy(i + 1, 1 - slot).start()

        acc_ref[...] = jnp.zeros_like(acc_ref)

    @pl.when(j < nj)
    def _():
        hx = hx_ref[slot]

        def conv(u, cw_ref, cb_ref):
            cw = cw_ref[...]
            y = (cw[0:1] * pltpu.roll(u, 2, 0) + cw[1:2] * pltpu.roll(u, 1, 0)
                 + cw[2:3] * u + cb_ref[...])
            return y[halo:, :]

        a = conv(_dot(hx, wa_ref[...]), cwa_ref, cba_ref)
        b = conv(_dot(hx, wb_ref[...]), cwb_ref, cbb_ref)
        acc_ref[...] += _dot((_silu(a) * b).astype(BF16), wd_ref[...])

    @pl.when(j >= nj)
    def _():
        rows = o_ref.shape[0]
        row0 = pl.multiple_of((j - nj) * rows, rows)
        x2 = x_ref[...] + gate_ref[...] * acc_ref[pl.ds(row0, rows), :]
        if final_norm:
            x2 = (x2 * lax.rsqrt(jnp.mean(x2 * x2, axis=-1, keepdims=True) + NORM_EPS)
                  * gf_ref[...])
        o_ref[...] = x2


def _ffn_call(x, h, gate, g_final, w_up, conv_w, conv_b, w_down, layer, final_norm):
    s, d = x.shape
    dff = w_down.shape[1]
    tm = min(s, 1024)
    tf = 512
    nj = dff // tf
    ro = min(tm, FFN_OUT_ROWS)
    n_out = tm // ro
    halo = BF16_SUBLANES
    assert CONV_W - 1 <= halo
    vec = pl.BlockSpec((1, d), lambda i, j: (0, 0))

    def cols(shape, off):
        return pl.BlockSpec((None,) + shape,
                            lambda i, j: (layer, 0, off + jnp.minimum(j, nj - 1)))

    def out_rows(i, j):
        return (jnp.maximum(i * n_out + jnp.maximum(j - nj, -1), 0), 0)

    conv_b3 = conv_b.reshape(conv_b.shape[0], 1, 2 * dff)
    return pl.pallas_call(
        functools.partial(_ffn_kernel, nj=nj, final_norm=final_norm),
        grid=(s // tm, nj + n_out),
        in_specs=[
            pl.BlockSpec(memory_space=pl.ANY),
            pl.BlockSpec((ro, d), out_rows),
            vec, vec,
            cols((d, tf), 0), cols((d, tf), nj),
            cols((CONV_W, tf), 0), cols((CONV_W, tf), nj),
            cols((1, tf), 0), cols((1, tf), nj),
            pl.BlockSpec((None, tf, d), lambda i, j: (layer, jnp.minimum(j, nj - 1), 0)),
        ],
        out_specs=pl.BlockSpec((ro, d), out_rows),
        out_shape=jax.ShapeDtypeStruct((s, d), F32),
        scratch_shapes=[pltpu.VMEM((2, tm + halo, d), BF16), pltpu.VMEM((tm, d), F32),
                        pltpu.SemaphoreType.DMA((2,))],
        compiler_params=_params(2),
        name="conv_ffn",
    )(h, x, gate, g_final.reshape(1, d), w_up, w_up, conv_w, conv_w, conv_b3, conv_b3, w_down)


def kernel(x, c, positions, w_ada, b_ada, g_norm1, w_in, w_ret_o, w_hg_o, w_out, hg_lb,
           g_norm2, w_up, conv_w, conv_b, w_down, g_final):
    batch, s, d = x.shape
    depth = w_in.shape[0]
    assert batch == 1 and s % CHUNK == 0

    xs = x.reshape(s, d)
    inv_freq = ROPE_BASE ** (-jnp.arange(0, RET_DK, 2, dtype=F32) / RET_DK)
    cos, sin = _rope_call(positions.reshape(s, 1), inv_freq.reshape(1, RET_DK // 2))

    w_in_b = w_in.astype(BF16)
    w_ret_o_b = w_ret_o.astype(BF16)
    w_hg_o_b = w_hg_o.astype(BF16)
    w_out_b = w_out.astype(BF16)

    for l in range(depth):
        mod = _mod_call(c.reshape(d, 1), w_ada, b_ada, l)
        shift1, scale1, gate1, shift2, scale2, gate2 = [
            mod[:, n * d:(n + 1) * d] for n in range(6)]
        dff = w_down.shape[1]
        p, w_up_b, w_down_b = _in_call(xs, g_norm1, scale1, shift1, w_in_b, l,
                                       w_up[l].reshape(2 * dff, d), w_down[l])
        w_up_b = w_up_b.reshape(1, d, 2 * dff)
        w_down_b = w_down_b.reshape(1, dff, d)
        xs, h2 = _mixmerge_call(p, cos, sin, hg_lb, xs, gate1, g_norm2, scale2, shift2,
                                w_ret_o_b, w_hg_o_b, w_out_b, l)
        xs = _ffn_call(xs, h2, gate2, g_final, w_up_b, conv_w[l:l + 1], conv_b[l:l + 1],
                       w_down_b, 0, final_norm=(l == depth - 1))
    return xs.reshape(batch, s, d)
```

```python
import functools

import numpy as np
import jax
import jax.numpy as jnp
from jax import lax
from jax.experimental import pallas as pl
from jax.experimental.pallas import tpu as pltpu

F32 = jnp.float32
BF16 = jnp.bfloat16

RET_H = 4
RET_DK = 256
RET_DV = 256
HG_H = 8
HG_DK = 128
HG_DV = 128
CHUNK = 128
CONV_W = 3
ROPE_BASE = 10000.0
NORM_EPS = 1e-6
HEAD_EPS = 1e-5

V7X_VMEM_BYTES = 64 * 1024 * 1024
VMEM_LIMIT = V7X_VMEM_BYTES - 6 * 1024 * 1024
BF16_SUBLANES = 16

F32_SUBLANES = 8
HG_LEVELS = 7
HG_MXU_LEVELS = (5,)


def _silu(t):
    return t * jax.nn.sigmoid(t)


def _dot(a, b):
    return jnp.dot(a, b, preferred_element_type=F32)


def _dot_nt(a, b):
    return lax.dot_general(a, b, (((1,), (1,)), ((), ())), preferred_element_type=F32)


def _dot_tn(a, b):
    return lax.dot_general(a, b, (((0,), (0,)), ((), ())), preferred_element_type=F32)


def _params(n_axes):
    return pltpu.CompilerParams(
        dimension_semantics=("arbitrary",) * n_axes, vmem_limit_bytes=VMEM_LIMIT)


def _mod_kernel(c_ref, w_ref, b_ref, o_ref):
    sc = _silu(c_ref[...])
    o_ref[...] = jnp.sum(sc * w_ref[...], axis=0, keepdims=True) + b_ref[...]


def _mod_call(c_col, w_ada, b_ada, layer):
    d, n = w_ada.shape[1], w_ada.shape[2]
    tn = 1024
    return pl.pallas_call(
        _mod_kernel,
        grid=(n // tn,),
        in_specs=[
            pl.BlockSpec((d, 1), lambda j: (0, 0)),
            pl.BlockSpec((None, d, tn), lambda j: (layer, 0, j)),
            pl.BlockSpec((None, 1, tn), lambda j: (layer, 0, j)),
        ],
        out_specs=pl.BlockSpec((1, tn), lambda j: (0, j)),
        out_shape=jax.ShapeDtypeStruct((1, n), F32),
        compiler_params=_params(1),
        name="mod",
    )(c_col, w_ada, b_ada.reshape(b_ada.shape[0], 1, n))


def _rope_kernel(pos_ref, f_ref, cos_ref, sin_ref):
    ang = pos_ref[...].astype(F32) * f_ref[...]
    cos_ref[...] = jnp.cos(ang)
    sin_ref[...] = jnp.sin(ang)


def _rope_call(pos_col, inv_freq):
    s = pos_col.shape[0]
    hd = inv_freq.shape[1]
    ts = min(s, 2048)
    return pl.pallas_call(
        _rope_kernel,
        grid=(s // ts,),
        in_specs=[pl.BlockSpec((ts, 1), lambda i: (i, 0)),
                  pl.BlockSpec((1, hd), lambda i: (0, 0))],
        out_specs=[pl.BlockSpec((ts, hd), lambda i: (i, 0))] * 2,
        out_shape=[jax.ShapeDtypeStruct((s, hd), F32)] * 2,
        compiler_params=_params(1),
        name="rope_table",
    )(pos_col, inv_freq)


def _norm_mod(x, g, scale, shift):
    y = x * lax.rsqrt(jnp.mean(x * x, axis=-1, keepdims=True) + NORM_EPS)
    return (y * g) * (1.0 + scale) + shift


def _in_kernel(x_ref, g_ref, sc_ref, sh_ref, w_ref, ca_hbm, cb_hbm, o_ref, ca_out, cb_out,
               h_even_ref, h_odd_ref, cin_a, cout_a, cin_b, cout_b, insem, outsem,
               *, layer, n_sub, n_a, n_b):
    i = pl.program_id(0)
    j = pl.program_id(1)
    nj = pl.num_programs(1)
    rb = x_ref.shape[0]
    n_chunks = n_a + n_b
    streams = ((0, n_a, ca_hbm, ca_out, cin_a, cout_a), (n_a, n_b, cb_hbm, cb_out, cin_b, cout_b))

    def norm_sub_block(h_ref):
        row0 = pl.multiple_of(jnp.minimum(j, n_sub - 1) * rb, rb)
        h_ref[pl.ds(row0, rb), :] = _norm_mod(
            x_ref[...], g_ref[...], sc_ref[...], sh_ref[...]).astype(BF16)

    def chunk_dma(k, write_back, op):
        slot = k % 2
        for first, count, src, dst, cin_ref, cout_ref in streams:
            @pl.when((k >= first) & (k < first + count))
            def _():
                rc = cin_ref.shape[1]
                rows = pl.ds(pl.multiple_of((k - first) * rc, BF16_SUBLANES), rc)
                if write_back:
                    copy = pltpu.make_async_copy(cout_ref.at[slot], dst.at[layer, rows, :],
                                                 outsem.at[slot])
                else:
                    copy = pltpu.make_async_copy(src.at[layer, rows, :], cin_ref.at[slot],
                                                 insem.at[slot])
                getattr(copy, op)()

    k = (i - 1) * nj + j

    @pl.when(i == 0)
    def _():
        norm_sub_block(h_even_ref)
        o_ref[...] = jnp.zeros_like(o_ref)

        @pl.when(j == nj - 1)
        def _():
            chunk_dma(k + 1, False, "start")

    @pl.when(i > 0)
    def _():
        chunk_dma(k, False, "wait")

        @pl.when(k + 1 < n_chunks)
        def _():
            chunk_dma(k + 1, False, "start")

        @pl.when(k >= 2)
        def _():
            chunk_dma(k - 2, True, "wait")

        for first, count, _, _, cin_ref, cout_ref in streams:
            @pl.when((k >= first) & (k < first + count))
            def _():
                cout_ref[k % 2] = cin_ref[k % 2].astype(BF16)

    @pl.when(i % 2 == 1)
    def _():
        norm_sub_block(h_odd_ref)
        o_ref[...] = _dot(h_even_ref[...], w_ref[...]).astype(o_ref.dtype)

    @pl.when((i > 0) & (i % 2 == 0))
    def _():
        norm_sub_block(h_even_ref)
        o_ref[...] = _dot(h_odd_ref[...], w_ref[...]).astype(o_ref.dtype)

    @pl.when(i > 0)
    def _():
        chunk_dma(k, True, "start")

        @pl.when(k == n_chunks - 1)
        def _():
            chunk_dma(k - 1, True, "wait")
            chunk_dma(k, True, "wait")


IN_TN = 2048
IN_ROW_BLOCK = 256


def _in_call(x, g, scale, shift, w, layer, cast_a, cast_b):
    s, d = x.shape
    tn = IN_TN
    nj = w.shape[2] // tn
    tm = min(s, 1024)
    rb = min(tm, IN_ROW_BLOCK)
    n_sub = tm // rb
    nt = s // tm
    assert n_sub <= nj
    n_chunks = nt * nj
    size_a, size_b = cast_a.shape[1] * cast_a.shape[2], cast_b.shape[1] * cast_b.shape[2]
    n_a = n_chunks * size_a // (size_a + size_b)
    n_b = n_chunks - n_a
    assert n_a >= 1 and n_b >= 1 and n_a * size_b == n_b * size_a
    rc_a, rc_b = cast_a.shape[1] // n_a, cast_b.shape[1] // n_b
    assert rc_a * n_a == cast_a.shape[1] and rc_b * n_b == cast_b.shape[1]
    assert rc_a % BF16_SUBLANES == 0 and rc_b % BF16_SUBLANES == 0
    vec = pl.BlockSpec((1, d), lambda i, j: (0, 0))
    hbm = pl.BlockSpec(memory_space=pl.ANY)

    def x_map(i, j):
        return (jnp.minimum(i, nt - 1) * n_sub + jnp.minimum(j, n_sub - 1), 0)

    return pl.pallas_call(
        functools.partial(_in_kernel, layer=layer, n_sub=n_sub, n_a=n_a, n_b=n_b),
        grid=(nt + 1, nj),
        in_specs=[
            pl.BlockSpec((rb, d), x_map),
            pl.BlockSpec((None, 1, d), lambda i, j: (layer, 0, 0)),
            vec, vec,
            pl.BlockSpec((None, d, tn), lambda i, j: (layer, 0, j)),
            hbm, hbm,
        ],
        out_specs=[pl.BlockSpec((tm, tn), lambda i, j: ((i + nt) % (nt + 1), j)), hbm, hbm],
        out_shape=[jax.ShapeDtypeStruct((s + tm, nj * tn), BF16),
                   jax.ShapeDtypeStruct(cast_a.shape, BF16),
                   jax.ShapeDtypeStruct(cast_b.shape, BF16)],
        scratch_shapes=[pltpu.VMEM((tm, d), BF16), pltpu.VMEM((tm, d), BF16),
                        pltpu.VMEM((2, rc_a, cast_a.shape[2]), F32),
                        pltpu.VMEM((2, rc_a, cast_a.shape[2]), BF16),
                        pltpu.VMEM((2, rc_b, cast_b.shape[2]), F32),
                        pltpu.VMEM((2, rc_b, cast_b.shape[2]), BF16),
                        pltpu.SemaphoreType.DMA((2,)), pltpu.SemaphoreType.DMA((2,))],
        compiler_params=_params(2),
        name="in_proj",
    )(x, g.reshape(g.shape[0], 1, d), scale, shift, w, cast_a, cast_b)


def _ret_tables():
    gamma = 1.0 - jnp.exp2(-5.0 - jnp.arange(RET_H, dtype=F32))
    log_g = jnp.log(gamma)
    idx = jnp.arange(CHUNK, dtype=F32)
    diff = idx[:, None] - idx[None, :]
    decay = jnp.where(diff[None] >= 0,
                      jnp.exp(jnp.maximum(diff, 0.0)[None] * log_g[:, None, None]), 0.0)
    xi = jnp.exp((idx[None, :] + 1.0) * log_g[:, None])[:, :, None]
    zeta = jnp.exp((CHUNK - 1.0 - idx[None, :]) * log_g[:, None])[:, :, None]
    g_chunk = jnp.exp(CHUNK * log_g)[:, None, None]
    return (decay, jnp.broadcast_to(xi, (RET_H, CHUNK, RET_DV)),
            jnp.broadcast_to(zeta, (RET_H, CHUNK, RET_DK)),
            jnp.broadcast_to(g_chunk, (RET_H, 1, RET_DV)))


def _hg_tables():
    t = np.arange(CHUNK)
    sums = [t[None, :] <= t[:, None]]
    upper, pair = [], []
    for l in range(HG_LEVELS):
        w = (CHUNK // 2) >> l
        up = (t // w) % 2 == 1
        upper.append(np.broadcast_to(up[:, None], (CHUNK, HG_DK)))
        pair.append(up[:, None] & ~up[None, :] & ((t[:, None] // (2 * w)) == (t[None, :] // (2 * w))))
        if l in HG_MXU_LEVELS:
            m = (t // (2 * w)) * (2 * w) + w - 1
            lo = np.minimum(t, m)[:, None]
            hi = np.maximum(t, m)[:, None]
            sums.append((t[None, :] > lo) & (t[None, :] <= hi))
    pair.append(t[:, None] == t[None, :])
    tsum = np.concatenate(sums, axis=0).astype(np.float32)
    return (jnp.asarray(tsum, BF16), jnp.asarray(np.stack(upper), F32),
            jnp.asarray(np.stack(pair), F32))


def _ret_head(h, row0, p_ref, cos, sin, dec_ref, xi_ref, zeta_ref, gch_ref, r_ref, o_ref):
    half = RET_DK // 2

    def cols(part):
        c0 = (part * RET_H + h) * RET_DK
        return p_ref[row0:row0 + CHUNK, c0:c0 + RET_DK]

    def rope(t):
        t1 = t[:, :half]
        t2 = t[:, half:]
        return jnp.concatenate([t1 * cos - t2 * sin, t2 * cos + t1 * sin], axis=-1)

    q = rope(cols(0).astype(F32))
    k = rope(cols(1).astype(F32)) * (RET_DK ** -0.5)
    v = cols(2)

    qb = q.astype(BF16)
    inner = _dot_nt(qb, k.astype(BF16)) * dec_ref[h]
    r = r_ref[h]
    o = _dot(inner.astype(BF16), v) + _dot(qb, r.astype(BF16)) * xi_ref[h]
    r_ref[h] = r * gch_ref[h] + _dot_tn((k * zeta_ref[h]).astype(BF16), v)

    oc = o - jnp.mean(o, axis=-1, keepdims=True)
    on = oc * lax.rsqrt(jnp.mean(oc * oc, axis=-1, keepdims=True) + HEAD_EPS)
    o_ref[row0:row0 + CHUNK, h * RET_DV:(h + 1) * RET_DV] = (
        on * _silu(cols(3).astype(F32))).astype(o_ref.dtype)


def _hg_head(h, row0, col0, p_ref, lb, tsum_ref, up_ref, pair_ref, st_ref, o_ref):
    def cols(part):
        c0 = col0 + (part * HG_H + h) * HG_DK
        return p_ref[row0:row0 + CHUNK, c0:c0 + HG_DK]

    q = _silu(cols(0).astype(F32))
    f = lb + (1.0 - lb) * jax.nn.sigmoid(cols(1).astype(F32))
    k = 1.0 - f
    v = cols(2)

    l2 = jnp.log2(f)
    l_hi = l2.astype(BF16)
    l_lo = (l2 - l_hi.astype(F32)).astype(BF16)
    sums2 = _dot(tsum_ref[...], jnp.concatenate([l_hi, l_lo], axis=1))
    sums = sums2[:, :HG_DK] + sums2[:, HG_DK:]
    b = sums[0:CHUNK]

    def rows(t, r0, n):
        return t[r0:r0 + n]

    g = F32_SUBLANES
    diag = pair_ref[HG_LEVELS] * jnp.sum(q * k, axis=-1, keepdims=True)
    a_rows = [rows(diag, r, g) for r in range(0, CHUNK, g)]
    qk = q - k
    mxu_level = 0
    for l in range(HG_LEVELS):
        w = (CHUNK // 2) >> l
        if w >= g:
            xs, ups = [], []
            for r0 in range(0, CHUNK, 2 * w):
                b_mid = rows(b, r0 + w - 1, 1)
                xs.append(rows(k, r0, w) * jnp.exp2(b_mid - rows(b, r0, w)))
                xs.append(rows(q, r0 + w, w) * jnp.exp2(rows(b, r0 + w, w) - b_mid))
                ups.append(xs[-1])
            x = jnp.concatenate(xs, axis=0).astype(BF16)
            if w >= BF16_SUBLANES:
                z = _dot_nt(jnp.concatenate(ups, axis=0).astype(BF16), x)
                up_starts = [r0 + w for r0 in range(0, CHUNK, 2 * w)]
                for n, r0 in enumerate(up_starts):
                    for r in range(0, w, g):
                        a_rows[(r0 + r) // g] = (a_rows[(r0 + r) // g]
                                                 + pair_ref[l, r0 + r:r0 + r + g, :]
                                                 * rows(z, n * w + r, g))
                continue
        elif l in HG_MXU_LEVELS:
            mxu_level += 1
            x = ((k + up_ref[l] * qk) * jnp.exp2(rows(sums, mxu_level * CHUNK, CHUNK))).astype(BF16)
        elif 2 * w == g:
            b_mid = jnp.concatenate(
                [jnp.broadcast_to(rows(b, r0 + w - 1, 1), (2 * w, HG_DK))
                 for r0 in range(0, CHUNK, 2 * w)], axis=0)
            sign = 2.0 * up_ref[l] - 1.0
            x = ((k + up_ref[l] * qk) * jnp.exp2((b - b_mid) * sign)).astype(BF16)
        else:
            assert w == 1
            x = (k + up_ref[l] * (q * f - k)).astype(BF16)
        z = _dot_nt(x, x)
        for r in range(0, CHUNK, g):
            a_rows[r // g] = a_rows[r // g] + pair_ref[l, r:r + g, :] * rows(z, r, g)
    a = jnp.concatenate(a_rows, axis=0)

    e_b = jnp.exp2(b)
    e_last = jnp.exp2(rows(b, CHUNK - 1, 1) - b)
    st = st_ref[h]
    o = _dot(a.astype(BF16), v) + _dot_nt((q * e_b).astype(BF16), st.astype(BF16))
    st_ref[h] = st * rows(e_b, CHUNK - 1, 1) + _dot_tn(v, (k * e_last).astype(BF16))

    on = o * lax.rsqrt(jnp.mean(o * o, axis=-1, keepdims=True) + HEAD_EPS)
    o_ref[row0:row0 + CHUNK, h * HG_DV:(h + 1) * HG_DV] = (
        on * _silu(cols(3).astype(F32))).astype(o_ref.dtype)


MIX_ROWS = 2 * CHUNK


def _mixers(layer, hg_col0, p_ref, cos_ref, sin_ref, dec_ref, xi_ref, zeta_ref, gch_ref,
            lbp_ref, tsum_ref, up_ref, pair_ref, r_ref, st_ref, oret_ref, ohg_ref):
    lbp = lbp_ref[...]
    e = jnp.exp(lbp - jnp.max(lbp, axis=0, keepdims=True))
    lb_all = jnp.sum(e[:layer + 1], axis=0, keepdims=True) / jnp.sum(e, axis=0, keepdims=True)
    for r0 in range(0, p_ref.shape[0], CHUNK):
        cos = cos_ref[r0:r0 + CHUNK, :]
        sin = sin_ref[r0:r0 + CHUNK, :]
        for h in range(RET_H):
            _ret_head(h, r0, p_ref, cos, sin, dec_ref, xi_ref, zeta_ref, gch_ref, r_ref,
                      oret_ref)
        for h in range(HG_H):
            _hg_head(h, r0, hg_col0, p_ref, lb_all[:, h * HG_DK:(h + 1) * HG_DK],
                     tsum_ref, up_ref, pair_ref, st_ref, ohg_ref)


def _merge(oret_ref, ohg_ref, gab_ref, x_ref, gate_ref, g2_ref, sc2_ref, sh2_ref,
           wa_ref, wb_ref, wo_ref, x1_ref, h2_ref):
    d = x_ref.shape[1]
    ya = _dot(oret_ref[...], wa_ref[...])
    yb = _dot(ohg_ref[...], wb_ref[...])
    merged = (jax.nn.sigmoid(gab_ref[:, :d].astype(F32)) * ya
              + jax.nn.sigmoid(gab_ref[:, d:].astype(F32)) * yb)
    x1 = x_ref[...] + gate_ref[...] * _dot(merged.astype(BF16), wo_ref[...])
    x1_ref[...] = x1
    h2_ref[...] = _norm_mod(x1, g2_ref[...], sc2_ref[...], sh2_ref[...]).astype(BF16)


def _mixmerge_kernel(layer, hg_col0, n_steps, p_ref, cos_ref, sin_ref, dec_ref, xi_ref,
                     zeta_ref, gch_ref, lbp_ref, tsum_ref, up_ref, pair_ref,
                     gab_ref, x_ref, gate_ref, g2_ref, sc2_ref, sh2_ref, wa_ref, wb_ref, wo_ref,
                     x1_ref, h2_ref, r_ref, st_ref, oa_even, ob_even, oa_odd, ob_odd):
    c = pl.program_id(0)

    @pl.when(c == 0)
    def _():
        r_ref[...] = jnp.zeros_like(r_ref)
        st_ref[...] = jnp.zeros_like(st_ref)

    mix_in = (p_ref, cos_ref, sin_ref, dec_ref, xi_ref, zeta_ref, gch_ref, lbp_ref, tsum_ref,
              up_ref, pair_ref, r_ref, st_ref)
    merge_in = (gab_ref, x_ref, gate_ref, g2_ref, sc2_ref, sh2_ref, wa_ref, wb_ref, wo_ref,
                x1_ref, h2_ref)
    bufs = ((oa_even, ob_even), (oa_odd, ob_odd))

    @pl.when(c == 0)
    def _():
        _mixers(layer, hg_col0, *mix_in, *bufs[0])

    for parity in (0, 1):
        @pl.when((c > 0) & (c < n_steps) & (c % 2 == parity))
        def _():
            _mixers(layer, hg_col0, *mix_in, *bufs[parity])
            _merge(*bufs[1 - parity], *merge_in)

    @pl.when(c == n_steps)
    def _():
        _merge(*bufs[(n_steps - 1) % 2], *merge_in)


def _mixmerge_call(p, cos, sin, hg_lb, x, gate, g2, scale2, shift2, w_ret_o, w_hg_o, w_out,
                   layer):
    s, d = x.shape
    tm = min(s, MIX_ROWS)
    n_steps = s // tm
    ret_cols = 4 * RET_H * RET_DK
    mix_cols = ret_cols + 4 * HG_H * HG_DK
    assert mix_cols % (2 * d) == 0
    gate_block = mix_cols // (2 * d)
    tables = _ret_tables() + (hg_lb,) + _hg_tables()

    def const(shape):
        return pl.BlockSpec(shape, lambda c: (0,) * len(shape), pipeline_mode=pl.Buffered(1))

    def layer_const(w):
        return pl.BlockSpec((None,) + w.shape[1:], lambda c: (layer,) + (0,) * (w.ndim - 1),
                            pipeline_mode=pl.Buffered(1))

    def cur(c):
        return jnp.minimum(c, n_steps - 1)

    def prev(c):
        return jnp.maximum(c - 1, 0)

    ret_w, hg_w = RET_H * RET_DV, HG_H * HG_DV
    return pl.pallas_call(
        functools.partial(_mixmerge_kernel, layer, ret_cols, n_steps),
        grid=(n_steps + 1,),
        in_specs=[pl.BlockSpec((tm, mix_cols), lambda c: (cur(c), 0)),
                  pl.BlockSpec((tm, RET_DK // 2), lambda c: (cur(c), 0)),
                  pl.BlockSpec((tm, RET_DK // 2), lambda c: (cur(c), 0))]
        + [const(t.shape) for t in tables]
        + [pl.BlockSpec((tm, 2 * d), lambda c: (prev(c), gate_block)),
           pl.BlockSpec((tm, d), lambda c: (prev(c), 0)),
           const((1, d)), layer_const(g2.reshape(g2.shape[0], 1, d)), const((1, d)),
           const((1, d)), layer_const(w_ret_o), layer_const(w_hg_o), layer_const(w_out)],
        out_specs=[pl.BlockSpec((tm, d), lambda c: (prev(c), 0))] * 2,
        out_shape=[jax.ShapeDtypeStruct((s, d), F32), jax.ShapeDtypeStruct((s, d), BF16)],
        scratch_shapes=[pltpu.VMEM((RET_H, RET_DK, RET_DV), F32),
                        pltpu.VMEM((HG_H, HG_DV, HG_DK), F32),
                        pltpu.VMEM((tm, ret_w), BF16), pltpu.VMEM((tm, hg_w), BF16),
                        pltpu.VMEM((tm, ret_w), BF16), pltpu.VMEM((tm, hg_w), BF16)],
        compiler_params=_params(1),
        name="mixers_merge",
    )(p, cos, sin, *tables, p, x, gate, g2.reshape(g2.shape[0], 1, d), scale2, shift2,
      w_ret_o, w_hg_o, w_out)


FFN_OUT_ROWS = 512


def _ffn_kernel(h_hbm, x_ref, gate_ref, gf_ref, wa_ref, wb_ref, cwa_ref, cwb_ref, cba_ref,
                cbb_ref, wd_ref, o_ref, hx_ref, acc_ref, hsem, *, nj, final_norm):
    i = pl.program_id(0)
    j = pl.program_id(1)
    nt = pl.num_programs(0)
    halo = BF16_SUBLANES
    tm = acc_ref.shape[0]
    slot = i % 2

    def first_tile_copy():
        return pltpu.make_async_copy(h_hbm.at[pl.ds(0, tm), :],
                                     hx_ref.at[0, pl.ds(halo, tm), :], hsem.at[0])

    def tile_copy(t, s):
        row0 = pl.multiple_of(t * tm - halo, halo)
        return pltpu.make_async_copy(h_hbm.at[pl.ds(row0, tm + halo), :], hx_ref.at[s],
                                     hsem.at[s])

    @pl.when(j == 0)
    def _():
        @pl.when(i == 0)
        def _():
            hx_ref[0, 0:halo, :] = jnp.zeros((halo, hx_ref.shape[2]), BF16)
            first_tile_copy().start()
            first_tile_copy().wait()

        @pl.when(i > 0)
        def _():
            tile_copy(i, slot).wait()

        @pl.when(i + 1 < nt)
        def _():
            tile_copy(i + 1, 1 - slot).start()

        acc_ref[...] = jnp.zeros_like(acc_ref)

    @pl.when(j < nj)
    def _():
        hx = hx_ref[slot]

        def conv(u, cw_ref, cb_ref):
            cw = cw_ref[...]
            y = (cw[0:1] * pltpu.roll(u, 2, 0) + cw[1:2] * pltpu.roll(u, 1, 0)
                 + cw[2:3] * u + cb_ref[...])
            return y[halo:, :]

        a = conv(_dot(hx, wa_ref[...]), cwa_ref, cba_ref)
        b = conv(_dot(hx, wb_ref[...]), cwb_ref, cbb_ref)
        acc_ref[...] += _dot((_silu(a) * b).astype(BF16), wd_ref[...])

    @pl.when(j >= nj)
    def _():
        rows = o_ref.shape[0]
        row0 = pl.multiple_of((j - nj) * rows, rows)
        x2 = x_ref[...] + gate_ref[...] * acc_ref[pl.ds(row0, rows), :]
        if final_norm:
            x2 = (x2 * lax.rsqrt(jnp.mean(x2 * x2, axis=-1, keepdims=True) + NORM_EPS)
                  * gf_ref[...])
        o_ref[...] = x2


def _ffn_call(x, h, gate, g_final, w_up, conv_w, conv_b, w_down, layer, final_norm):
    s, d = x.shape
    dff = w_down.shape[1]
    tm = min(s, 1024)
    tf = 512
    nj = dff // tf
    ro = min(tm, FFN_OUT_ROWS)
    n_out = tm // ro
    halo = BF16_SUBLANES
    assert CONV_W - 1 <= halo
    vec = pl.BlockSpec((1, d), lambda i, j: (0, 0))

    def cols(shape, off):
        return pl.BlockSpec((None,) + shape,
                            lambda i, j: (layer, 0, off + jnp.minimum(j, nj - 1)))

    def out_rows(i, j):
        return (jnp.maximum(i * n_out + jnp.maximum(j - nj, -1), 0), 0)

    conv_b3 = conv_b.reshape(conv_b.shape[0], 1, 2 * dff)
    return pl.pallas_call(
        functools.partial(_ffn_kernel, nj=nj, final_norm=final_norm),
        grid=(s // tm, nj + n_out),
        in_specs=[
            pl.BlockSpec(memory_space=pl.ANY),
            pl.BlockSpec((ro, d), out_rows),
            vec, vec,
            cols((d, tf), 0), cols((d, tf), nj),
            cols((CONV_W, tf), 0), cols((CONV_W, tf), nj),
            cols((1, tf), 0), cols((1, tf), nj),
            pl.BlockSpec((None, tf, d), lambda i, j: (layer, jnp.minimum(j, nj - 1), 0)),
        ],
        out_specs=pl.BlockSpec((ro, d), out_rows),
        out_shape=jax.ShapeDtypeStruct((s, d), F32),
        scratch_shapes=[pltpu.VMEM((2, tm + halo, d), BF16), pltpu.VMEM((tm, d), F32),
                        pltpu.SemaphoreType.DMA((2,))],
        compiler_params=_params(2),
        name="conv_ffn",
    )(h, x, gate, g_final.reshape(1, d), w_up, w_up, conv_w, conv_w, conv_b3, conv_b3, w_down)


def kernel(x, c, positions, w_ada, b_ada, g_norm1, w_in, w_ret_o, w_hg_o, w_out, hg_lb,
           g_norm2, w_up, conv_w, conv_b, w_down, g_final):
    batch, s, d = x.shape
    depth = w_in.shape[0]
    assert batch == 1 and s % CHUNK == 0

    xs = x.reshape(s, d)
    inv_freq = ROPE_BASE ** (-jnp.arange(0, RET_DK, 2, dtype=F32) / RET_DK)
    cos, sin = _rope_call(positions.reshape(s, 1), inv_freq.reshape(1, RET_DK // 2))

    w_in_b = w_in.astype(BF16)
    w_ret_o_b = w_ret_o.astype(BF16)
    w_hg_o_b = w_hg_o.astype(BF16)
    w_out_b = w_out.astype(BF16)

    for l in range(depth):
        mod = _mod_call(c.reshape(d, 1), w_ada, b_ada, l)
        shift1, scale1, gate1, shift2, scale2, gate2 = [
            mod[:, n * d:(n + 1) * d] for n in range(6)]
        p, w_up_b, w_down_b = _in_call(xs, g_norm1, scale1, shift1, w_in_b, l, w_up, w_down)
        xs, h2 = _mixmerge_call(p, cos, sin, hg_lb, xs, gate1, g_norm2, scale2, shift2,
                                w_ret_o_b, w_hg_o_b, w_out_b, l)
        xs = _ffn_call(xs, h2, gate2, g_final, w_up_b, conv_w, conv_b, w_down_b, l,
                       final_norm=(l == depth - 1))
    return xs.reshape(batch, s, d)
```

```python
import functools

import numpy as np
import jax
import jax.numpy as jnp
from jax import lax
from jax.experimental import pallas as pl
from jax.experimental.pallas import tpu as pltpu

F32 = jnp.float32
BF16 = jnp.bfloat16

RET_H = 4
RET_DK = 256
RET_DV = 256
HG_H = 8
HG_DK = 128
HG_DV = 128
CHUNK = 128
CONV_W = 3
ROPE_BASE = 10000.0
NORM_EPS = 1e-6
HEAD_EPS = 1e-5

V7X_VMEM_BYTES = 64 * 1024 * 1024
VMEM_LIMIT = V7X_VMEM_BYTES - 6 * 1024 * 1024
BF16_SUBLANES = 16

F32_SUBLANES = 8
HG_LEVELS = 7
HG_MXU_LEVELS = (5,)


def _silu(t):
    return t * jax.nn.sigmoid(t)


def _dot(a, b):
    return jnp.dot(a, b, preferred_element_type=F32)


def _dot_nt(a, b):
    return lax.dot_general(a, b, (((1,), (1,)), ((), ())), preferred_element_type=F32)


def _dot_tn(a, b):
    return lax.dot_general(a, b, (((0,), (0,)), ((), ())), preferred_element_type=F32)


def _params(n_axes):
    return pltpu.CompilerParams(
        dimension_semantics=("arbitrary",) * n_axes, vmem_limit_bytes=VMEM_LIMIT)


def _mod_kernel(c_ref, w_ref, b_ref, o_ref):
    sc = _silu(c_ref[...])
    o_ref[...] = jnp.sum(sc * w_ref[...], axis=0, keepdims=True) + b_ref[...]


def _mod_call(c_col, w_ada, b_ada, layer):
    d, n = w_ada.shape[1], w_ada.shape[2]
    tn = 1024
    return pl.pallas_call(
        _mod_kernel,
        grid=(n // tn,),
        in_specs=[
            pl.BlockSpec((d, 1), lambda j: (0, 0)),
            pl.BlockSpec((None, d, tn), lambda j: (layer, 0, j)),
            pl.BlockSpec((None, 1, tn), lambda j: (layer, 0, j)),
        ],
        out_specs=pl.BlockSpec((1, tn), lambda j: (0, j)),
        out_shape=jax.ShapeDtypeStruct((1, n), F32),
        compiler_params=_params(1),
        name="mod",
    )(c_col, w_ada, b_ada.reshape(b_ada.shape[0], 1, n))


def _rope_kernel(pos_ref, f_ref, w_hbm, cos_ref, sin_ref, wb_hbm, cin_ref, cout_ref, insem,
                 outsem):
    k = pl.program_id(0)
    n = pl.num_programs(0)
    rc = cin_ref.shape[1]
    chunks_per_layer = w_hbm.shape[1] // rc

    def chunk_dma(c, write_back, op):
        slot = c % 2
        layer = c // chunks_per_layer
        rows = pl.ds(pl.multiple_of((c % chunks_per_layer) * rc, BF16_SUBLANES), rc)
        if write_back:
            copy = pltpu.make_async_copy(cout_ref.at[slot], wb_hbm.at[layer, rows, :],
                                         outsem.at[slot])
        else:
            copy = pltpu.make_async_copy(w_hbm.at[layer, rows, :], cin_ref.at[slot],
                                         insem.at[slot])
        getattr(copy, op)()

    @pl.when(k == 0)
    def _():
        chunk_dma(k, False, "start")

    chunk_dma(k, False, "wait")

    @pl.when(k + 1 < n)
    def _():
        chunk_dma(k + 1, False, "start")

    @pl.when(k >= 2)
    def _():
        chunk_dma(k - 2, True, "wait")

    cout_ref[k % 2] = cin_ref[k % 2].astype(BF16)
    chunk_dma(k, True, "start")

    ang = pos_ref[...].astype(F32) * f_ref[...]
    cos_ref[...] = jnp.cos(ang)
    sin_ref[...] = jnp.sin(ang)

    @pl.when(k == n - 1)
    def _():
        @pl.when(k >= 1)
        def _():
            chunk_dma(k - 1, True, "wait")

        chunk_dma(k, True, "wait")


def _rope_call(pos_col, inv_freq, w):
    s = pos_col.shape[0]
    hd = inv_freq.shape[1]
    ts = min(s, 1024)
    n = s // ts
    rc = w.shape[0] * w.shape[1] // n
    assert rc * n == w.shape[0] * w.shape[1] and w.shape[1] % rc == 0
    assert rc % BF16_SUBLANES == 0
    hbm = pl.BlockSpec(memory_space=pl.ANY)
    return pl.pallas_call(
        _rope_kernel,
        grid=(n,),
        in_specs=[pl.BlockSpec((ts, 1), lambda i: (i, 0)),
                  pl.BlockSpec((1, hd), lambda i: (0, 0)), hbm],
        out_specs=[pl.BlockSpec((ts, hd), lambda i: (i, 0))] * 2 + [hbm],
        out_shape=[jax.ShapeDtypeStruct((s, hd), F32)] * 2
        + [jax.ShapeDtypeStruct(w.shape, BF16)],
        scratch_shapes=[pltpu.VMEM((2, rc, w.shape[2]), F32),
                        pltpu.VMEM((2, rc, w.shape[2]), BF16),
                        pltpu.SemaphoreType.DMA((2,)), pltpu.SemaphoreType.DMA((2,))],
        compiler_params=_params(1),
        name="rope_table",
    )(pos_col, inv_freq, w)


def _norm_mod(x, g, scale, shift):
    y = x * lax.rsqrt(jnp.mean(x * x, axis=-1, keepdims=True) + NORM_EPS)
    return (y * g) * (1.0 + scale) + shift


def _in_kernel(x_ref, g_ref, sc_ref, sh_ref, w_ref, ca_hbm, cb_hbm, o_ref, ca_out, cb_out,
               h_even_ref, h_odd_ref, cin_a, cout_a, cin_b, cout_b, insem, outsem,
               *, layer, n_sub, n_a, n_b):
    i = pl.program_id(0)
    j = pl.program_id(1)
    nj = pl.num_programs(1)
    rb = x_ref.shape[0]
    n_chunks = n_a + n_b
    streams = ((0, n_a, ca_hbm, ca_out, cin_a, cout_a), (n_a, n_b, cb_hbm, cb_out, cin_b, cout_b))

    def norm_sub_block(h_ref):
        row0 = pl.multiple_of(jnp.minimum(j, n_sub - 1) * rb, rb)
        h_ref[pl.ds(row0, rb), :] = _norm_mod(
            x_ref[...], g_ref[...], sc_ref[...], sh_ref[...]).astype(BF16)

    def chunk_dma(k, write_back, op):
        slot = k % 2
        for first, count, src, dst, cin_ref, cout_ref in streams:
            @pl.when((k >= first) & (k < first + count))
            def _():
                rc = cin_ref.shape[1]
                rows = pl.ds(pl.multiple_of((k - first) * rc, BF16_SUBLANES), rc)
                if write_back:
                    copy = pltpu.make_async_copy(cout_ref.at[slot], dst.at[layer, rows, :],
                                                 outsem.at[slot])
                else:
                    copy = pltpu.make_async_copy(src.at[layer, rows, :], cin_ref.at[slot],
                                                 insem.at[slot])
                getattr(copy, op)()

    k = (i - 1) * nj + j

    @pl.when(i == 0)
    def _():
        norm_sub_block(h_even_ref)
        o_ref[...] = jnp.zeros_like(o_ref)

        @pl.when(j == nj - 1)
        def _():
            chunk_dma(k + 1, False, "start")

    @pl.when(i > 0)
    def _():
        chunk_dma(k, False, "wait")

        @pl.when(k + 1 < n_chunks)
        def _():
            chunk_dma(k + 1, False, "start")

        @pl.when(k >= 2)
        def _():
            chunk_dma(k - 2, True, "wait")

        for first, count, _, _, cin_ref, cout_ref in streams:
            @pl.when((k >= first) & (k < first + count))
            def _():
                cout_ref[k % 2] = cin_ref[k % 2].astype(BF16)

    @pl.when(i % 2 == 1)
    def _():
        norm_sub_block(h_odd_ref)
        o_ref[...] = _dot(h_even_ref[...], w_ref[...]).astype(o_ref.dtype)

    @pl.when((i > 0) & (i % 2 == 0))
    def _():
        norm_sub_block(h_even_ref)
        o_ref[...] = _dot(h_odd_ref[...], w_ref[...]).astype(o_ref.dtype)

    @pl.when(i > 0)
    def _():
        chunk_dma(k, True, "start")

        @pl.when(k == n_chunks - 1)
        def _():
            chunk_dma(k - 1, True, "wait")
            chunk_dma(k, True, "wait")


IN_TN = 2048
IN_ROW_BLOCK = 256


def _in_call(x, g, scale, shift, w, layer, cast_a, cast_b):
    s, d = x.shape
    tn = IN_TN
    nj = w.shape[2] // tn
    tm = min(s, 1024)
    rb = min(tm, IN_ROW_BLOCK)
    n_sub = tm // rb
    nt = s // tm
    assert n_sub <= nj
    n_chunks = nt * nj
    size_a, size_b = cast_a.shape[1] * cast_a.shape[2], cast_b.shape[1] * cast_b.shape[2]
    n_a = n_chunks * size_a // (size_a + size_b)
    n_b = n_chunks - n_a
    assert n_a >= 1 and n_b >= 1 and n_a * size_b == n_b * size_a
    rc_a, rc_b = cast_a.shape[1] // n_a, cast_b.shape[1] // n_b
    assert rc_a * n_a == cast_a.shape[1] and rc_b * n_b == cast_b.shape[1]
    assert rc_a % BF16_SUBLANES == 0 and rc_b % BF16_SUBLANES == 0
    vec = pl.BlockSpec((1, d), lambda i, j: (0, 0))
    hbm = pl.BlockSpec(memory_space=pl.ANY)

    def x_map(i, j):
        return (jnp.minimum(i, nt - 1) * n_sub + jnp.minimum(j, n_sub - 1), 0)

    return pl.pallas_call(
        functools.partial(_in_kernel, layer=layer, n_sub=n_sub, n_a=n_a, n_b=n_b),
        grid=(nt + 1, nj),
        in_specs=[
            pl.BlockSpec((rb, d), x_map),
            pl.BlockSpec((None, 1, d), lambda i, j: (layer, 0, 0)),
            vec, vec,
            pl.BlockSpec((None, d, tn), lambda i, j: (layer, 0, j)),
            hbm, hbm,
        ],
        out_specs=[pl.BlockSpec((tm, tn), lambda i, j: ((i + nt) % (nt + 1), j)), hbm, hbm],
        out_shape=[jax.ShapeDtypeStruct((s + tm, nj * tn), BF16),
                   jax.ShapeDtypeStruct(cast_a.shape, BF16),
                   jax.ShapeDtypeStruct(cast_b.shape, BF16)],
        scratch_shapes=[pltpu.VMEM((tm, d), BF16), pltpu.VMEM((tm, d), BF16),
                        pltpu.VMEM((2, rc_a, cast_a.shape[2]), F32),
                        pltpu.VMEM((2, rc_a, cast_a.shape[2]), BF16),
                        pltpu.VMEM((2, rc_b, cast_b.shape[2]), F32),
                        pltpu.VMEM((2, rc_b, cast_b.shape[2]), BF16),
                        pltpu.SemaphoreType.DMA((2,)), pltpu.SemaphoreType.DMA((2,))],
        compiler_params=_params(2),
        name="in_proj",
    )(x, g.reshape(g.shape[0], 1, d), scale, shift, w, cast_a, cast_b)


def _ret_tables():
    gamma = 1.0 - jnp.exp2(-5.0 - jnp.arange(RET_H, dtype=F32))
    log_g = jnp.log(gamma)
    idx = jnp.arange(CHUNK, dtype=F32)
    diff = idx[:, None] - idx[None, :]
    decay = jnp.where(diff[None] >= 0,
                      jnp.exp(jnp.maximum(diff, 0.0)[None] * log_g[:, None, None]), 0.0)
    xi = jnp.exp((idx[None, :] + 1.0) * log_g[:, None])[:, :, None]
    zeta = jnp.exp((CHUNK - 1.0 - idx[None, :]) * log_g[:, None])[:, :, None]
    g_chunk = jnp.exp(CHUNK * log_g)[:, None, None]
    return (decay, jnp.broadcast_to(xi, (RET_H, CHUNK, RET_DV)),
            jnp.broadcast_to(zeta, (RET_H, CHUNK, RET_DK)),
            jnp.broadcast_to(g_chunk, (RET_H, 1, RET_DV)))


def _hg_tables():
    t = np.arange(CHUNK)
    sums = [t[None, :] <= t[:, None]]
    upper, pair = [], []
    for l in range(HG_LEVELS):
        w = (CHUNK // 2) >> l
        up = (t // w) % 2 == 1
        upper.append(np.broadcast_to(up[:, None], (CHUNK, HG_DK)))
        pair.append(up[:, None] & ~up[None, :] & ((t[:, None] // (2 * w)) == (t[None, :] // (2 * w))))
        if l in HG_MXU_LEVELS:
            m = (t // (2 * w)) * (2 * w) + w - 1
            lo = np.minimum(t, m)[:, None]
            hi = np.maximum(t, m)[:, None]
            sums.append((t[None, :] > lo) & (t[None, :] <= hi))
    pair.append(t[:, None] == t[None, :])
    tsum = np.concatenate(sums, axis=0).astype(np.float32)
    return (jnp.asarray(tsum, BF16), jnp.asarray(np.stack(upper), F32),
            jnp.asarray(np.stack(pair), F32))


def _ret_head(h, row0, p_ref, cos, sin, dec_ref, xi_ref, zeta_ref, gch_ref, r_ref, o_ref):
    half = RET_DK // 2

    def cols(part):
        c0 = (part * RET_H + h) * RET_DK
        return p_ref[row0:row0 + CHUNK, c0:c0 + RET_DK]

    def rope(t):
        t1 = t[:, :half]
        t2 = t[:, half:]
        return jnp.concatenate([t1 * cos - t2 * sin, t2 * cos + t1 * sin], axis=-1)

    q = rope(cols(0).astype(F32))
    k = rope(cols(1).astype(F32)) * (RET_DK ** -0.5)
    v = cols(2)

    qb = q.astype(BF16)
    inner = _dot_nt(qb, k.astype(BF16)) * dec_ref[h]
    r = r_ref[h]
    o = _dot(inner.astype(BF16), v) + _dot(qb, r.astype(BF16)) * xi_ref[h]
    r_ref[h] = r * gch_ref[h] + _dot_tn((k * zeta_ref[h]).astype(BF16), v)

    oc = o - jnp.mean(o, axis=-1, keepdims=True)
    on = oc * lax.rsqrt(jnp.mean(oc * oc, axis=-1, keepdims=True) + HEAD_EPS)
    o_ref[row0:row0 + CHUNK, h * RET_DV:(h + 1) * RET_DV] = (
        on * _silu(cols(3).astype(F32))).astype(o_ref.dtype)


def _hg_head(h, row0, col0, p_ref, lb, tsum_ref, up_ref, pair_ref, st_ref, o_ref):
    def cols(part):
        c0 = col0 + (part * HG_H + h) * HG_DK
        return p_ref[row0:row0 + CHUNK, c0:c0 + HG_DK]

    q = _silu(cols(0).astype(F32))
    f = lb + (1.0 - lb) * jax.nn.sigmoid(cols(1).astype(F32))
    k = 1.0 - f
    v = cols(2)

    l2 = jnp.log2(f)
    l_hi = l2.astype(BF16)
    l_lo = (l2 - l_hi.astype(F32)).astype(BF16)
    sums2 = _dot(tsum_ref[...], jnp.concatenate([l_hi, l_lo], axis=1))
    sums = sums2[:, :HG_DK] + sums2[:, HG_DK:]
    b = sums[0:CHUNK]

    def rows(t, r0, n):
        return t[r0:r0 + n]

    g = F32_SUBLANES
    diag = pair_ref[HG_LEVELS] * jnp.sum(q * k, axis=-1, keepdims=True)
    a_rows = [rows(diag, r, g) for r in range(0, CHUNK, g)]
    qk = q - k
    mxu_level = 0
    for l in range(HG_LEVELS):
        w = (CHUNK // 2) >> l
        if w >= g:
            xs, ups = [], []
            for r0 in range(0, CHUNK, 2 * w):
                b_mid = rows(b, r0 + w - 1, 1)
                xs.append(rows(k, r0, w) * jnp.exp2(b_mid - rows(b, r0, w)))
                xs.append(rows(q, r0 + w, w) * jnp.exp2(rows(b, r0 + w, w) - b_mid))
                ups.append(xs[-1])
            x = jnp.concatenate(xs, axis=0).astype(BF16)
            if w >= BF16_SUBLANES:
                z = _dot_nt(jnp.concatenate(ups, axis=0).astype(BF16), x)
                up_starts = [r0 + w for r0 in range(0, CHUNK, 2 * w)]
                for n, r0 in enumerate(up_starts):
                    for r in range(0, w, g):
                        a_rows[(r0 + r) // g] = (a_rows[(r0 + r) // g]
                                                 + pair_ref[l, r0 + r:r0 + r + g, :]
                                                 * rows(z, n * w + r, g))
                continue
        elif l in HG_MXU_LEVELS:
            mxu_level += 1
            x = ((k + up_ref[l] * qk) * jnp.exp2(rows(sums, mxu_level * CHUNK, CHUNK))).astype(BF16)
        elif 2 * w == g:
            b_mid = jnp.concatenate(
                [jnp.broadcast_to(rows(b, r0 + w - 1, 1), (2 * w, HG_DK))
                 for r0 in range(0, CHUNK, 2 * w)], axis=0)
            sign = 2.0 * up_ref[l] - 1.0
            x = ((k + up_ref[l] * qk) * jnp.exp2((b - b_mid) * sign)).astype(BF16)
        else:
            assert w == 1
            x = (k + up_ref[l] * (q * f - k)).astype(BF16)
        z = _dot_nt(x, x)
        for r in range(0, CHUNK, g):
            a_rows[r // g] = a_rows[r // g] + pair_ref[l, r:r + g, :] * rows(z, r, g)
    a = jnp.concatenate(a_rows, axis=0)

    e_b = jnp.exp2(b)
    e_last = jnp.exp2(rows(b, CHUNK - 1, 1) - b)
    st = st_ref[h]
    o = _dot(a.astype(BF16), v) + _dot_nt((q * e_b).astype(BF16), st.astype(BF16))
    st_ref[h] = st * rows(e_b, CHUNK - 1, 1) + _dot_tn(v, (k * e_last).astype(BF16))

    on = o * lax.rsqrt(jnp.mean(o * o, axis=-1, keepdims=True) + HEAD_EPS)
    o_ref[row0:row0 + CHUNK, h * HG_DV:(h + 1) * HG_DV] = (
        on * _silu(cols(3).astype(F32))).astype(o_ref.dtype)


MIX_ROWS = 2 * CHUNK


def _mixers(layer, hg_col0, p_ref, cos_ref, sin_ref, dec_ref, xi_ref, zeta_ref, gch_ref,
            lbp_ref, tsum_ref, up_ref, pair_ref, r_ref, st_ref, oret_ref, ohg_ref):
    lbp = lbp_ref[...]
    e = jnp.exp(lbp - jnp.max(lbp, axis=0, keepdims=True))
    lb_all = jnp.sum(e[:layer + 1], axis=0, keepdims=True) / jnp.sum(e, axis=0, keepdims=True)
    for r0 in range(0, p_ref.shape[0], CHUNK):
        cos = cos_ref[r0:r0 + CHUNK, :]
        sin = sin_ref[r0:r0 + CHUNK, :]
        for h in range(RET_H):
            _ret_head(h, r0, p_ref, cos, sin, dec_ref, xi_ref, zeta_ref, gch_ref, r_ref,
                      oret_ref)
        for h in range(HG_H):
            _hg_head(h, r0, hg_col0, p_ref, lb_all[:, h * HG_DK:(h + 1) * HG_DK],
                     tsum_ref, up_ref, pair_ref, st_ref, ohg_ref)


def _merge(oret_ref, ohg_ref, gab_ref, x_ref, gate_ref, g2_ref, sc2_ref, sh2_ref,
           wa_ref, wb_ref, wo_ref, x1_ref, h2_ref):
    d = x_ref.shape[1]
    ya = _dot(oret_ref[...], wa_ref[...])
    yb = _dot(ohg_ref[...], wb_ref[...])
    merged = (jax.nn.sigmoid(gab_ref[:, :d].astype(F32)) * ya
              + jax.nn.sigmoid(gab_ref[:, d:].astype(F32)) * yb)
    x1 = x_ref[...] + gate_ref[...] * _dot(merged.astype(BF16), wo_ref[...])
    x1_ref[...] = x1
    h2_ref[...] = _norm_mod(x1, g2_ref[...], sc2_ref[...], sh2_ref[...]).astype(BF16)


def _mixmerge_kernel(layer, hg_col0, n_steps, p_ref, cos_ref, sin_ref, dec_ref, xi_ref,
                     zeta_ref, gch_ref, lbp_ref, tsum_ref, up_ref, pair_ref,
                     gab_ref, x_ref, gate_ref, g2_ref, sc2_ref, sh2_ref, wa_ref, wb_ref, wo_ref,
                     x1_ref, h2_ref, r_ref, st_ref, oa_even, ob_even, oa_odd, ob_odd):
    c = pl.program_id(0)

    @pl.when(c == 0)
    def _():
        r_ref[...] = jnp.zeros_like(r_ref)
        st_ref[...] = jnp.zeros_like(st_ref)

    mix_in = (p_ref, cos_ref, sin_ref, dec_ref, xi_ref, zeta_ref, gch_ref, lbp_ref, tsum_ref,
              up_ref, pair_ref, r_ref, st_ref)
    merge_in = (gab_ref, x_ref, gate_ref, g2_ref, sc2_ref, sh2_ref, wa_ref, wb_ref, wo_ref,
                x1_ref, h2_ref)
    bufs = ((oa_even, ob_even), (oa_odd, ob_odd))

    @pl.when(c == 0)
    def _():
        _mixers(layer, hg_col0, *mix_in, *bufs[0])

    for parity in (0, 1):
        @pl.when((c > 0) & (c < n_steps) & (c % 2 == parity))
        def _():
            _mixers(layer, hg_col0, *mix_in, *bufs[parity])
            _merge(*bufs[1 - parity], *merge_in)

    @pl.when(c == n_steps)
    def _():
        _merge(*bufs[(n_steps - 1) % 2], *merge_in)


def _mixmerge_call(p, cos, sin, hg_lb, x, gate, g2, scale2, shift2, w_ret_o, w_hg_o, w_out,
                   layer):
    s, d = x.shape
    tm = min(s, MIX_ROWS)
    n_steps = s // tm
    ret_cols = 4 * RET_H * RET_DK
    mix_cols = ret_cols + 4 * HG_H * HG_DK
    assert mix_cols % (2 * d) == 0
    gate_block = mix_cols // (2 * d)
    tables = _ret_tables() + (hg_lb,) + _hg_tables()

    def const(shape):
        return pl.BlockSpec(shape, lambda c: (0,) * len(shape), pipeline_mode=pl.Buffered(1))

    def layer_const(w):
        return pl.BlockSpec((None,) + w.shape[1:], lambda c: (layer,) + (0,) * (w.ndim - 1),
                            pipeline_mode=pl.Buffered(1))

    def cur(c):
        return jnp.minimum(c, n_steps - 1)

    def prev(c):
        return jnp.maximum(c - 1, 0)

    ret_w, hg_w = RET_H * RET_DV, HG_H * HG_DV
    return pl.pallas_call(
        functools.partial(_mixmerge_kernel, layer, ret_cols, n_steps),
        grid=(n_steps + 1,),
        in_specs=[pl.BlockSpec((tm, mix_cols), lambda c: (cur(c), 0)),
                  pl.BlockSpec((tm, RET_DK // 2), lambda c: (cur(c), 0)),
                  pl.BlockSpec((tm, RET_DK // 2), lambda c: (cur(c), 0))]
        + [const(t.shape) for t in tables]
        + [pl.BlockSpec((tm, 2 * d), lambda c: (prev(c), gate_block)),
           pl.BlockSpec((tm, d), lambda c: (prev(c), 0)),
           const((1, d)), layer_const(g2.reshape(g2.shape[0], 1, d)), const((1, d)),
           const((1, d)), layer_const(w_ret_o), layer_const(w_hg_o), layer_const(w_out)],
        out_specs=[pl.BlockSpec((tm, d), lambda c: (prev(c), 0))] * 2,
        out_shape=[jax.ShapeDtypeStruct((s, d), F32), jax.ShapeDtypeStruct((s, d), BF16)],
        scratch_shapes=[pltpu.VMEM((RET_H, RET_DK, RET_DV), F32),
                        pltpu.VMEM((HG_H, HG_DV, HG_DK), F32),
                        pltpu.VMEM((tm, ret_w), BF16), pltpu.VMEM((tm, hg_w), BF16),
                        pltpu.VMEM((tm, ret_w), BF16), pltpu.VMEM((tm, hg_w), BF16)],
        compiler_params=_params(1),
        name="mixers_merge",
    )(p, cos, sin, *tables, p, x, gate, g2.reshape(g2.shape[0], 1, d), scale2, shift2,
      w_ret_o, w_hg_o, w_out)


FFN_OUT_ROWS = 512


def _ffn_kernel(h_hbm, x_ref, gate_ref, gf_ref, wa_ref, wb_ref, cwa_ref, cwb_ref, cba_ref,
                cbb_ref, wd_ref, o_ref, hx_ref, acc_ref, hsem, *, nj, final_norm):
    i = pl.program_id(0)
    j = pl.program_id(1)
    nt = pl.num_programs(0)
    halo = BF16_SUBLANES
    tm = acc_ref.shape[0]
    slot = i % 2

    def first_tile_copy():
        return pltpu.make_async_copy(h_hbm.at[pl.ds(0, tm), :],
                                     hx_ref.at[0, pl.ds(halo, tm), :], hsem.at[0])

    def tile_copy(t, s):
        row0 = pl.multiple_of(t * tm - halo, halo)
        return pltpu.make_async_copy(h_hbm.at[pl.ds(row0, tm + halo), :], hx_ref.at[s],
                                     hsem.at[s])

    @pl.when(j == 0)
    def _():
        @pl.when(i == 0)
        def _():
            hx_ref[0, 0:halo, :] = jnp.zeros((halo, hx_ref.shape[2]), BF16)
            first_tile_copy().start()
            first_tile_copy().wait()

        @pl.when(i > 0)
        def _():
            tile_copy(i, slot).wait()

        @pl.when(i + 1 < nt)
        def _():
            tile_copy(i + 1, 1 - slot).start()

        acc_ref[...] = jnp.zeros_like(acc_ref)

    @pl.when(j < nj)
    def _():
        hx = hx_ref[slot]

        def conv(u, cw_ref, cb_ref):
            cw = cw_ref[...]
            y = (cw[0:1] * pltpu.roll(u, 2, 0) + cw[1:2] * pltpu.roll(u, 1, 0)
                 + cw[2:3] * u + cb_ref[...])
            return y[halo:, :]

        a = conv(_dot(hx, wa_ref[...]), cwa_ref, cba_ref)
        b = conv(_dot(hx, wb_ref[...]), cwb_ref, cbb_ref)
        acc_ref[...] += _dot((_silu(a) * b).astype(BF16), wd_ref[...])

    @pl.when(j >= nj)
    def _():
        rows = o_ref.shape[0]
        row0 = pl.multiple_of((j - nj) * rows, rows)
        x2 = x_ref[...] + gate_ref[...] * acc_ref[pl.ds(row0, rows), :]
        if final_norm:
            x2 = (x2 * lax.rsqrt(jnp.mean(x2 * x2, axis=-1, keepdims=True) + NORM_EPS)
                  * gf_ref[...])
        o_ref[...] = x2


def _ffn_call(x, h, gate, g_final, w_up, conv_w, conv_b, w_down, layer, final_norm):
    s, d = x.shape
    dff = w_down.shape[1]
    tm = min(s, 1024)
    tf = 512
    nj = dff // tf
    ro = min(tm, FFN_OUT_ROWS)
    n_out = tm // ro
    halo = BF16_SUBLANES
    assert CONV_W - 1 <= halo
    vec = pl.BlockSpec((1, d), lambda i, j: (0, 0))

    def cols(shape, off):
        return pl.BlockSpec((None,) + shape,
                            lambda i, j: (layer, 0, off + jnp.minimum(j, nj - 1)))

    def out_rows(i, j):
        return (jnp.maximum(i * n_out + jnp.maximum(j - nj, -1), 0), 0)

    conv_b3 = conv_b.reshape(conv_b.shape[0], 1, 2 * dff)
    return pl.pallas_call(
        functools.partial(_ffn_kernel, nj=nj, final_norm=final_norm),
        grid=(s // tm, nj + n_out),
        in_specs=[
            pl.BlockSpec(memory_space=pl.ANY),
            pl.BlockSpec((ro, d), out_rows),
            vec, vec,
            cols((d, tf), 0), cols((d, tf), nj),
            cols((CONV_W, tf), 0), cols((CONV_W, tf), nj),
            cols((1, tf), 0), cols((1, tf), nj),
            pl.BlockSpec((None, tf, d), lambda i, j: (layer, jnp.minimum(j, nj - 1), 0)),
        ],
        out_specs=pl.BlockSpec((ro, d), out_rows),
        out_shape=jax.ShapeDtypeStruct((s, d), F32),
        scratch_shapes=[pltpu.VMEM((2, tm + halo, d), BF16), pltpu.VMEM((tm, d), F32),
                        pltpu.SemaphoreType.DMA((2,))],
        compiler_params=_params(2),
        name="conv_ffn",
    )(h, x, gate, g_final.reshape(1, d), w_up, w_up, conv_w, conv_w, conv_b3, conv_b3, w_down)


def kernel(x, c, positions, w_ada, b_ada, g_norm1, w_in, w_ret_o, w_hg_o, w_out, hg_lb,
           g_norm2, w_up, conv_w, conv_b, w_down, g_final):
    batch, s, d = x.shape
    depth = w_in.shape[0]
    assert batch == 1 and s % CHUNK == 0

    xs = x.reshape(s, d)
    inv_freq = ROPE_BASE ** (-jnp.arange(0, RET_DK, 2, dtype=F32) / RET_DK)
    cos, sin, w_in_b = _rope_call(positions.reshape(s, 1), inv_freq.reshape(1, RET_DK // 2),
                                  w_in)
    w_ret_o_b = w_ret_o.astype(BF16)
    w_hg_o_b = w_hg_o.astype(BF16)
    w_out_b = w_out.astype(BF16)

    for l in range(depth):
        mod = _mod_call(c.reshape(d, 1), w_ada, b_ada, l)
        shift1, scale1, gate1, shift2, scale2, gate2 = [
            mod[:, n * d:(n + 1) * d] for n in range(6)]
        p, w_up_b, w_down_b = _in_call(xs, g_norm1, scale1, shift1, w_in_b, l, w_up, w_down)
        xs, h2 = _mixmerge_call(p, cos, sin, hg_lb, xs, gate1, g_norm2, scale2, shift2,
                                w_ret_o_b, w_hg_o_b, w_out_b, l)
        xs = _ffn_call(xs, h2, gate2, g_final, w_up_b, conv_w, conv_b, w_down_b, l,
                       final_norm=(l == depth - 1))
    return xs.reshape(batch, s, d)
```
